```python
import math
import jax, jax.numpy as jnp
from jax import lax
import numpy as np

D_MODEL = 1024
BATCH = 8
SEQ = 2048
DEPTH = 1
DEC_BATCH = 16
DEC_SEQ = 4096
PAST_LEN = 128

GRID_W = 64
MIX_WIDTH = D_MODEL
NA_HEADS = 8
NA_HEAD_DIM = 64
NA_WIDTH = NA_HEADS * NA_HEAD_DIM
NA_ROWS = 8
NA_COLS = 16
RET_HEADS = 4
RET_HEAD_DIM = 128
RET_WIDTH = RET_HEADS * RET_HEAD_DIM
RET_CHUNK = 128
ROPE_BASE = 10000.0
IN_COLS = 3 * NA_WIDTH + 4 * RET_WIDTH
PEER_HEADS = 8
PEER_QDIM = 256
PEER_HALF = PEER_QDIM // 2
N_KEYS = 128
N_EXPERTS = N_KEYS * N_KEYS
PEER_TOPK = 16
PEER_BLOCK = 128
EPS = 1e-6

kernel_name = 'hymba_na_retnet_peer_encoder'


def rmsnorm(x, g):
    xf = x.astype(jnp.float32)
    y = xf * lax.rsqrt(jnp.mean(xf * xf, axis=-1, keepdims=True) + EPS)
    return (y * g.astype(jnp.float32)).astype(x.dtype)


def rotary(x):
    L, d = x.shape[1], x.shape[-1]
    inv = 1.0 / (ROPE_BASE ** (jnp.arange(0, d, 2, dtype=jnp.float32) / d))
    ang = jnp.arange(L, dtype=jnp.float32)[:, None] * inv[None, :]
    cos = jnp.cos(ang)[None, :, None, :]
    sin = jnp.sin(ang)[None, :, None, :]
    x1 = x[..., : d // 2].astype(jnp.float32)
    x2 = x[..., d // 2:].astype(jnp.float32)
    return jnp.concatenate([x1 * cos - x2 * sin, x1 * sin + x2 * cos], axis=-1).astype(x.dtype)


def neighbourhood_attention(q, k, v, rpb):
    B, L, H, dh = q.shape
    rows = L // GRID_W
    kr = min(NA_ROWS, rows)
    qg = q.reshape(B, rows, GRID_W, H, dh)
    kg = k.reshape(B, rows, GRID_W, H, dh)
    vg = v.reshape(B, rows, GRID_W, H, dh)
    r_idx = jnp.arange(rows)
    row_start = jnp.clip(r_idx - kr // 2, 0, rows - kr)
    c_idx = jnp.arange(GRID_W)
    col_start = jnp.clip(c_idx - NA_COLS // 2, 0, GRID_W - NA_COLS)
    col_idx = col_start[:, None] + jnp.arange(NA_COLS)[None, :]
    col_bias_idx = col_idx - c_idx[:, None] + (NA_COLS - 1)
    scale = dh ** -0.5

    def one_row(args):
        qr, rs, ri = args
        k_band = lax.dynamic_slice_in_dim(kg, rs, kr, axis=1)
        v_band = lax.dynamic_slice_in_dim(vg, rs, kr, axis=1)
        k_nb = k_band[:, :, col_idx]
        v_nb = v_band[:, :, col_idx]
        s = jnp.einsum('bqhd,brqjhd->bhqrj', qr, k_nb).astype(jnp.float32) * scale
        row_bias_idx = rs + jnp.arange(kr) - ri + (NA_ROWS - 1)
        bias = rpb[:, row_bias_idx][:, :, col_bias_idx]
        s = s + jnp.transpose(bias, (0, 2, 1, 3)).astype(jnp.float32)[None]
        p = jax.nn.softmax(s.reshape(B, H, GRID_W, kr * NA_COLS), axis=-1)
        p = p.reshape(B, H, GRID_W, kr, NA_COLS).astype(v.dtype)
        return jnp.einsum('bhqrj,brqjhd->bqhd', p, v_nb)

    out = lax.map(one_row, (jnp.transpose(qg, (1, 0, 2, 3, 4)), row_start, r_idx))
    return jnp.transpose(out, (1, 0, 2, 3, 4)).reshape(B, L, H * dh)


def chunk_retention(q, k, v, log_gamma, strict):
    B, L, H, dk = q.shape
    dv = v.shape[-1]
    C = RET_CHUNK
    nc = L // C
    dt = q.dtype
    qc = q.reshape(B, nc, C, H, dk)
    kc = k.reshape(B, nc, C, H, dk)
    vc = v.reshape(B, nc, C, H, dv)
    pos = jnp.arange(C, dtype=jnp.float32)
    diff = pos[:, None] - pos[None, :]
    mask = (diff > 0) if strict else (diff >= 0)
    decay = jnp.where(mask[None], jnp.exp(log_gamma[:, None, None] * jnp.maximum(diff, 0.0)[None]), 0.0).astype(dt)
    k_decay = jnp.exp(log_gamma[None, :] * (C - 1 - pos)[:, None]).astype(dt)
    q_decay = jnp.exp(log_gamma[None, :] * (pos + 1)[:, None]).astype(dt)
    chunk_decay = jnp.exp(log_gamma * C).astype(dt)
    att = jnp.einsum('bnqhd,bnkhd->bnhqk', qc, kc) * decay[None, None]
    o_intra = jnp.einsum('bnhqk,bnkhe->bnqhe', att, vc)
    S = jnp.einsum('bnkhd,kh,bnkhe->nbhde', kc, k_decay, vc)

    def step(R, S_i):
        return chunk_decay[None, :, None, None] * R + S_i, R

    _, R_prev = lax.scan(step, jnp.zeros(S.shape[1:], S.dtype), S)
    o_inter = jnp.einsum('bnqhd,qh,nbhde->bnqhe', qc, q_decay, R_prev)
    return (o_intra + o_inter).reshape(B, L, H, dv)


def bidirectional_retention(q, k, v, gate, decay_fwd, decay_bwd, norm_g):
    B, L, H, dk = q.shape
    q = rotary(q)
    k = rotary(k) * (dk ** -0.5)
    lg_f = jax.nn.log_sigmoid(decay_fwd.astype(jnp.float32))
    lg_b = jax.nn.log_sigmoid(decay_bwd.astype(jnp.float32))
    o_f = chunk_retention(q, k, v, lg_f, False)
    o_b = jnp.flip(chunk_retention(jnp.flip(q, 1), jnp.flip(k, 1), jnp.flip(v, 1), lg_b, True), 1)
    o = (o_f + o_b).astype(jnp.float32)
    mu = jnp.mean(o, axis=-1, keepdims=True)
    var = jnp.mean(jnp.square(o - mu), axis=-1, keepdims=True)
    o = (o - mu) * lax.rsqrt(var + EPS)
    o = o.reshape(B, L, H * v.shape[-1]) * norm_g.astype(jnp.float32)
    return (jax.nn.silu(gate.astype(jnp.float32)) * o).astype(v.dtype)


def peer(h, w_q, sub_keys, u, v):
    B, L, D = h.shape
    T = B * L
    ht = h.reshape(T, D)
    q = (ht @ w_q).reshape(T, PEER_HEADS, 2, PEER_HALF)
    s = jnp.einsum('thpd,hpnd->thpn', q, sub_keys)
    s_top, i_top = lax.top_k(s, PEER_TOPK)
    cand = s_top[:, :, 0, :, None] + s_top[:, :, 1, None, :]
    cand_idx = i_top[:, :, 0, :, None] * N_KEYS + i_top[:, :, 1, None, :]
    best_s, best_pos = lax.top_k(cand.reshape(T, PEER_HEADS, PEER_TOPK * PEER_TOPK), PEER_TOPK)
    experts = jnp.take_along_axis(cand_idx.reshape(T, PEER_HEADS, PEER_TOPK * PEER_TOPK), best_pos, axis=-1)
    gates = jax.nn.softmax(best_s.astype(jnp.float32), axis=-1).astype(h.dtype)
    nb = T // PEER_BLOCK
    xb = ht.reshape(nb, PEER_BLOCK, D)
    eb = experts.reshape(nb, PEER_BLOCK, PEER_HEADS * PEER_TOPK)
    gb = gates.reshape(nb, PEER_BLOCK, PEER_HEADS * PEER_TOPK)

    def block(args):
        xt, e, g = args
        ue = u[e]
        a = jax.nn.gelu(jnp.einsum('td,tkd->tk', xt, ue))
        ve = v[e]
        return jnp.einsum('tk,tkd->td', g * a, ve)

    return lax.map(block, (xb, eb, gb)).reshape(B, L, D)


def encoder_layer(x, norm1_g, w_in, na_rpb, ret_decay_fwd, ret_decay_bwd, ret_norm_g,
                  w_out, norm2_g, peer_w_q, peer_sub_keys, peer_u, peer_v):
    B, L, D = x.shape
    h = rmsnorm(x, norm1_g)
    proj = h @ w_in
    cuts = [NA_WIDTH, 2 * NA_WIDTH, 3 * NA_WIDTH,
            3 * NA_WIDTH + RET_WIDTH, 3 * NA_WIDTH + 2 * RET_WIDTH, 3 * NA_WIDTH + 3 * RET_WIDTH]
    na_q, na_k, na_v, r_q, r_k, r_v, r_g = jnp.split(proj, cuts, axis=-1)
    na_shape = (B, L, NA_HEADS, NA_HEAD_DIM)
    na_o = neighbourhood_attention(na_q.reshape(na_shape), na_k.reshape(na_shape),
                                   na_v.reshape(na_shape), na_rpb)
    r_shape = (B, L, RET_HEADS, RET_HEAD_DIM)
    ret_o = bidirectional_retention(r_q.reshape(r_shape), r_k.reshape(r_shape), r_v.reshape(r_shape),
                                    r_g, ret_decay_fwd, ret_decay_bwd, ret_norm_g)
    x = x + jnp.concatenate([na_o, ret_o], axis=-1) @ w_out
    x = x + peer(rmsnorm(x, norm2_g), peer_w_q, peer_sub_keys, peer_u, peer_v)
    return x


def setup_inputs(seed: int = 0) -> dict:
    key = jax.random.key(seed)
    ks = jax.random.split(key, 16)
    f32 = jnp.float32
    base = 1.0 - 2.0 ** (-5.0 - jnp.arange(RET_HEADS, dtype=f32))
    logit = jnp.log(base) - jnp.log1p(-base)
    return {
        'x_prompt': jax.random.normal(ks[0], (BATCH, SEQ, D_MODEL), f32),
        'x_sample': jax.random.normal(ks[1], (DEC_BATCH, DEC_SEQ, D_MODEL), f32),
        'norm1_g': 1.0 + 0.02 * jax.random.normal(ks[2], (DEPTH, D_MODEL), f32),
        'w_in': jax.random.normal(ks[3], (DEPTH, D_MODEL, IN_COLS), f32) * D_MODEL ** -0.5,
        'na_rpb': 0.1 * jax.random.normal(ks[4], (DEPTH, NA_HEADS, 2 * NA_ROWS - 1, 2 * NA_COLS - 1), f32),
        'ret_decay_fwd': logit[None, :] + 0.1 * jax.random.normal(ks[5], (DEPTH, RET_HEADS), f32),
        'ret_decay_bwd': logit[None, :] + 0.1 * jax.random.normal(ks[6], (DEPTH, RET_HEADS), f32),
        'ret_norm_g': 1.0 + 0.02 * jax.random.normal(ks[7], (DEPTH, RET_WIDTH), f32),
        'w_out': jax.random.normal(ks[8], (DEPTH, MIX_WIDTH, D_MODEL), f32) * MIX_WIDTH ** -0.5,
        'norm2_g': 1.0 + 0.02 * jax.random.normal(ks[9], (DEPTH, D_MODEL), f32),
        'peer_w_q': jax.random.normal(ks[10], (DEPTH, D_MODEL, PEER_HEADS * PEER_QDIM), f32) * D_MODEL ** -0.5,
        'peer_sub_keys': jax.random.normal(ks[11], (DEPTH, PEER_HEADS, 2, N_KEYS, PEER_HALF), f32) * PEER_HALF ** -0.5,
        'peer_u': jax.random.normal(ks[12], (DEPTH, N_EXPERTS, D_MODEL), f32) * D_MODEL ** -0.5,
        'peer_v': 0.5 * jax.random.normal(ks[13], (DEPTH, N_EXPERTS, D_MODEL), f32),
        'final_g': 1.0 + 0.02 * jax.random.normal(ks[14], (D_MODEL,), f32),
    }


def reference(x_prompt, x_sample, norm1_g, w_in, na_rpb, ret_decay_fwd, ret_decay_bwd, ret_norm_g,
              w_out, norm2_g, peer_w_q, peer_sub_keys, peer_u, peer_v, final_g):
    def trunk(x):
        for l in range(DEPTH):
            x = encoder_layer(x, norm1_g[l], w_in[l], na_rpb[l], ret_decay_fwd[l], ret_decay_bwd[l],
                              ret_norm_g[l], w_out[l], norm2_g[l], peer_w_q[l], peer_sub_keys[l],
                              peer_u[l], peer_v[l])
        return rmsnorm(x, final_g)

    y_prompt = trunk(x_prompt)
    y_sample = trunk(x_sample)
    return (y_prompt, y_sample)
```

```python
import math
import jax, jax.numpy as jnp
from jax import lax
import numpy as np
from jax.experimental import pallas as pl

D_MODEL = 1024
DEPTH = 1
GRID_W = 64
NA_HEADS = 8
NA_HEAD_DIM = 64
NA_WIDTH = NA_HEADS * NA_HEAD_DIM
NA_ROWS = 8
NA_COLS = 16
RET_HEADS = 4
RET_HEAD_DIM = 128
RET_WIDTH = RET_HEADS * RET_HEAD_DIM
RET_CHUNK = 128
ROPE_BASE = 10000.0
PEER_HEADS = 8
PEER_QDIM = 256
PEER_HALF = PEER_QDIM // 2
N_KEYS = 128
N_EXPERTS = N_KEYS * N_KEYS
PEER_TOPK = 16
PEER_BLOCK = 128
EPS = 1e-6


def rmsnorm(x, g):
    xf = x.astype(jnp.float32)
    y = xf * lax.rsqrt(jnp.mean(xf * xf, axis=-1, keepdims=True) + EPS)
    return (y * g.astype(jnp.float32)).astype(x.dtype)


def rotary(x):
    L, d = x.shape[1], x.shape[-1]
    inv = 1.0 / (ROPE_BASE ** (jnp.arange(0, d, 2, dtype=jnp.float32) / d))
    ang = jnp.arange(L, dtype=jnp.float32)[:, None] * inv[None, :]
    cos = jnp.cos(ang)[None, :, None, :]
    sin = jnp.sin(ang)[None, :, None, :]
    x1 = x[..., : d // 2].astype(jnp.float32)
    x2 = x[..., d // 2:].astype(jnp.float32)
    return jnp.concatenate([x1 * cos - x2 * sin, x1 * sin + x2 * cos], axis=-1).astype(x.dtype)


def neighbourhood_attention(q, k, v, rpb):
    B, L, H, dh = q.shape
    rows = L // GRID_W
    kr = min(NA_ROWS, rows)
    qg = q.reshape(B, rows, GRID_W, H, dh)
    kg = k.reshape(B, rows, GRID_W, H, dh)
    vg = v.reshape(B, rows, GRID_W, H, dh)
    r_idx = jnp.arange(rows)
    row_start = jnp.clip(r_idx - kr // 2, 0, rows - kr)
    c_idx = jnp.arange(GRID_W)
    col_start = jnp.clip(c_idx - NA_COLS // 2, 0, GRID_W - NA_COLS)
    col_idx = col_start[:, None] + jnp.arange(NA_COLS)[None, :]
    col_bias_idx = col_idx - c_idx[:, None] + (NA_COLS - 1)
    scale = dh ** -0.5

    def one_row(args):
        qr, rs, ri = args
        k_band = lax.dynamic_slice_in_dim(kg, rs, kr, axis=1)
        v_band = lax.dynamic_slice_in_dim(vg, rs, kr, axis=1)
        k_nb = k_band[:, :, col_idx]
        v_nb = v_band[:, :, col_idx]
        s = jnp.einsum('bqhd,brqjhd->bhqrj', qr, k_nb).astype(jnp.float32) * scale
        row_bias_idx = rs + jnp.arange(kr) - ri + (NA_ROWS - 1)
        bias = rpb[:, row_bias_idx][:, :, col_bias_idx]
        s = s + jnp.transpose(bias, (0, 2, 1, 3)).astype(jnp.float32)[None]
        p = jax.nn.softmax(s.reshape(B, H, GRID_W, kr * NA_COLS), axis=-1)
        p = p.reshape(B, H, GRID_W, kr, NA_COLS).astype(v.dtype)
        return jnp.einsum('bhqrj,brqjhd->bqhd', p, v_nb)

    out = lax.map(one_row, (jnp.transpose(qg, (1, 0, 2, 3, 4)), row_start, r_idx))
    return jnp.transpose(out, (1, 0, 2, 3, 4)).reshape(B, L, H * dh)


def chunk_retention(q, k, v, log_gamma, strict):
    B, L, H, dk = q.shape
    dv = v.shape[-1]
    C = RET_CHUNK
    nc = L // C
    dt = q.dtype
    qc = q.reshape(B, nc, C, H, dk)
    kc = k.reshape(B, nc, C, H, dk)
    vc = v.reshape(B, nc, C, H, dv)
    pos = jnp.arange(C, dtype=jnp.float32)
    diff = pos[:, None] - pos[None, :]
    mask = (diff > 0) if strict else (diff >= 0)
    decay = jnp.where(mask[None], jnp.exp(log_gamma[:, None, None] * jnp.maximum(diff, 0.0)[None]), 0.0).astype(dt)
    k_decay = jnp.exp(log_gamma[None, :] * (C - 1 - pos)[:, None]).astype(dt)
    q_decay = jnp.exp(log_gamma[None, :] * (pos + 1)[:, None]).astype(dt)
    chunk_decay = jnp.exp(log_gamma * C).astype(dt)
    att = jnp.einsum('bnqhd,bnkhd->bnhqk', qc, kc) * decay[None, None]
    o_intra = jnp.einsum('bnhqk,bnkhe->bnqhe', att, vc)
    S = jnp.einsum('bnkhd,kh,bnkhe->nbhde', kc, k_decay, vc)

    def step(R, S_i):
        return chunk_decay[None, :, None, None] * R + S_i, R

    _, R_prev = lax.scan(step, jnp.zeros(S.shape[1:], S.dtype), S)
    o_inter = jnp.einsum('bnqhd,qh,nbhde->bnqhe', qc, q_decay, R_prev)
    return (o_intra + o_inter).reshape(B, L, H, dv)


def bidirectional_retention(q, k, v, gate, decay_fwd, decay_bwd, norm_g):
    B, L, H, dk = q.shape
    q = rotary(q)
    k = rotary(k) * (dk ** -0.5)
    lg_f = jax.nn.log_sigmoid(decay_fwd.astype(jnp.float32))
    lg_b = jax.nn.log_sigmoid(decay_bwd.astype(jnp.float32))
    o_f = chunk_retention(q, k, v, lg_f, False)
    o_b = jnp.flip(chunk_retention(jnp.flip(q, 1), jnp.flip(k, 1), jnp.flip(v, 1), lg_b, True), 1)
    o = (o_f + o_b).astype(jnp.float32)
    mu = jnp.mean(o, axis=-1, keepdims=True)
    var = jnp.mean(jnp.square(o - mu), axis=-1, keepdims=True)
    o = (o - mu) * lax.rsqrt(var + EPS)
    o = o.reshape(B, L, H * v.shape[-1]) * norm_g.astype(jnp.float32)
    return (jax.nn.silu(gate.astype(jnp.float32)) * o).astype(v.dtype)


def peer(h, w_q, sub_keys, u, v):
    B, L, D = h.shape
    T = B * L
    ht = h.reshape(T, D)
    q = (ht @ w_q).reshape(T, PEER_HEADS, 2, PEER_HALF)
    s = jnp.einsum('thpd,hpnd->thpn', q, sub_keys)
    s_top, i_top = lax.top_k(s, PEER_TOPK)
    cand = s_top[:, :, 0, :, None] + s_top[:, :, 1, None, :]
    cand_idx = i_top[:, :, 0, :, None] * N_KEYS + i_top[:, :, 1, None, :]
    best_s, best_pos = lax.top_k(cand.reshape(T, PEER_HEADS, PEER_TOPK * PEER_TOPK), PEER_TOPK)
    experts = jnp.take_along_axis(cand_idx.reshape(T, PEER_HEADS, PEER_TOPK * PEER_TOPK), best_pos, axis=-1)
    gates = jax.nn.softmax(best_s.astype(jnp.float32), axis=-1).astype(h.dtype)
    nb = T // PEER_BLOCK
    xb = ht.reshape(nb, PEER_BLOCK, D)
    eb = experts.reshape(nb, PEER_BLOCK, PEER_HEADS * PEER_TOPK)
    gb = gates.reshape(nb, PEER_BLOCK, PEER_HEADS * PEER_TOPK)

    def block(args):
        xt, e, g = args
        ue = u[e]
        a = jax.nn.gelu(jnp.einsum('td,tkd->tk', xt, ue))
        ve = v[e]
        return jnp.einsum('tk,tkd->td', g * a, ve)

    return lax.map(block, (xb, eb, gb)).reshape(B, L, D)


def encoder_layer(x, norm1_g, w_in, na_rpb, ret_decay_fwd, ret_decay_bwd, ret_norm_g,
                  w_out, norm2_g, peer_w_q, peer_sub_keys, peer_u, peer_v):
    B, L, D = x.shape
    h = rmsnorm(x, norm1_g)
    proj = h @ w_in
    cuts = [NA_WIDTH, 2 * NA_WIDTH, 3 * NA_WIDTH,
            3 * NA_WIDTH + RET_WIDTH, 3 * NA_WIDTH + 2 * RET_WIDTH, 3 * NA_WIDTH + 3 * RET_WIDTH]
    na_q, na_k, na_v, r_q, r_k, r_v, r_g = jnp.split(proj, cuts, axis=-1)
    na_shape = (B, L, NA_HEADS, NA_HEAD_DIM)
    na_o = neighbourhood_attention(na_q.reshape(na_shape), na_k.reshape(na_shape),
                                   na_v.reshape(na_shape), na_rpb)
    r_shape = (B, L, RET_HEADS, RET_HEAD_DIM)
    ret_o = bidirectional_retention(r_q.reshape(r_shape), r_k.reshape(r_shape), r_v.reshape(r_shape),
                                    r_g, ret_decay_fwd, ret_decay_bwd, ret_norm_g)
    x = x + jnp.concatenate([na_o, ret_o], axis=-1) @ w_out
    x = x + peer(rmsnorm(x, norm2_g), peer_w_q, peer_sub_keys, peer_u, peer_v)
    return x


def _final_norm_kernel(x_ref, g_ref, o_ref):
    x = x_ref[...]
    y = x * lax.rsqrt(jnp.mean(x * x, axis=-1, keepdims=True) + EPS)
    o_ref[...] = y * g_ref[...]


def _final_norm(x, g):
    B, L, D = x.shape
    T = B * L
    xt = x.reshape(T, D)
    out = pl.pallas_call(
        _final_norm_kernel,
        out_shape=jax.ShapeDtypeStruct((T, D), jnp.float32),
        grid=(T // 1024,),
        in_specs=[pl.BlockSpec((1024, D), lambda i: (i, 0)), pl.BlockSpec((1, D), lambda i: (0, 0))],
        out_specs=pl.BlockSpec((1024, D), lambda i: (i, 0)),
    )(xt, g.reshape(1, D))
    return out.reshape(B, L, D)


def kernel(x_prompt, x_sample, norm1_g, w_in, na_rpb, ret_decay_fwd, ret_decay_bwd, ret_norm_g,
           w_out, norm2_g, peer_w_q, peer_sub_keys, peer_u, peer_v, final_g):
    def trunk(x):
        for l in range(DEPTH):
            x = encoder_layer(x, norm1_g[l], w_in[l], na_rpb[l], ret_decay_fwd[l], ret_decay_bwd[l],
                              ret_norm_g[l], w_out[l], norm2_g[l], peer_w_q[l], peer_sub_keys[l],
                              peer_u[l], peer_v[l])
        return _final_norm(x, final_g)

    return (trunk(x_prompt), trunk(x_sample))
```

```python
import math
import jax, jax.numpy as jnp
from jax import lax
import numpy as np
from jax.experimental import pallas as pl
from jax.experimental.pallas import tpu as pltpu

D_MODEL = 1024
DEPTH = 1
GRID_W = 64
NA_HEADS = 8
NA_HEAD_DIM = 64
NA_WIDTH = NA_HEADS * NA_HEAD_DIM
NA_ROWS = 8
NA_COLS = 16
RET_HEADS = 4
RET_HEAD_DIM = 128
RET_WIDTH = RET_HEADS * RET_HEAD_DIM
RET_CHUNK = 128
ROPE_BASE = 10000.0
PEER_HEADS = 8
PEER_QDIM = 256
PEER_HALF = PEER_QDIM // 2
N_KEYS = 128
N_EXPERTS = N_KEYS * N_KEYS
PEER_TOPK = 16
PEER_BLOCK = 128
EPS = 1e-6


def rmsnorm(x, g):
    xf = x.astype(jnp.float32)
    y = xf * lax.rsqrt(jnp.mean(xf * xf, axis=-1, keepdims=True) + EPS)
    return (y * g.astype(jnp.float32)).astype(x.dtype)


def rotary(x):
    L, d = x.shape[1], x.shape[-1]
    inv = 1.0 / (ROPE_BASE ** (jnp.arange(0, d, 2, dtype=jnp.float32) / d))
    ang = jnp.arange(L, dtype=jnp.float32)[:, None] * inv[None, :]
    cos = jnp.cos(ang)[None, :, None, :]
    sin = jnp.sin(ang)[None, :, None, :]
    x1 = x[..., : d // 2].astype(jnp.float32)
    x2 = x[..., d // 2:].astype(jnp.float32)
    return jnp.concatenate([x1 * cos - x2 * sin, x1 * sin + x2 * cos], axis=-1).astype(x.dtype)


def neighbourhood_attention(q, k, v, rpb):
    B, L, H, dh = q.shape
    rows = L // GRID_W
    kr = min(NA_ROWS, rows)
    qg = q.reshape(B, rows, GRID_W, H, dh)
    kg = k.reshape(B, rows, GRID_W, H, dh)
    vg = v.reshape(B, rows, GRID_W, H, dh)
    r_idx = jnp.arange(rows)
    row_start = jnp.clip(r_idx - kr // 2, 0, rows - kr)
    c_idx = jnp.arange(GRID_W)
    col_start = jnp.clip(c_idx - NA_COLS // 2, 0, GRID_W - NA_COLS)
    col_idx = col_start[:, None] + jnp.arange(NA_COLS)[None, :]
    col_bias_idx = col_idx - c_idx[:, None] + (NA_COLS - 1)
    scale = dh ** -0.5

    def one_row(args):
        qr, rs, ri = args
        k_band = lax.dynamic_slice_in_dim(kg, rs, kr, axis=1)
        v_band = lax.dynamic_slice_in_dim(vg, rs, kr, axis=1)
        k_nb = k_band[:, :, col_idx]
        v_nb = v_band[:, :, col_idx]
        s = jnp.einsum('bqhd,brqjhd->bhqrj', qr, k_nb).astype(jnp.float32) * scale
        row_bias_idx = rs + jnp.arange(kr) - ri + (NA_ROWS - 1)
        bias = rpb[:, row_bias_idx][:, :, col_bias_idx]
        s = s + jnp.transpose(bias, (0, 2, 1, 3)).astype(jnp.float32)[None]
        p = jax.nn.softmax(s.reshape(B, H, GRID_W, kr * NA_COLS), axis=-1)
        p = p.reshape(B, H, GRID_W, kr, NA_COLS).astype(v.dtype)
        return jnp.einsum('bhqrj,brqjhd->bqhd', p, v_nb)

    out = lax.map(one_row, (jnp.transpose(qg, (1, 0, 2, 3, 4)), row_start, r_idx))
    return jnp.transpose(out, (1, 0, 2, 3, 4)).reshape(B, L, H * dh)


def chunk_retention(q, k, v, log_gamma, strict):
    B, L, H, dk = q.shape
    dv = v.shape[-1]
    C = RET_CHUNK
    nc = L // C
    dt = q.dtype
    qc = q.reshape(B, nc, C, H, dk)
    kc = k.reshape(B, nc, C, H, dk)
    vc = v.reshape(B, nc, C, H, dv)
    pos = jnp.arange(C, dtype=jnp.float32)
    diff = pos[:, None] - pos[None, :]
    mask = (diff > 0) if strict else (diff >= 0)
    decay = jnp.where(mask[None], jnp.exp(log_gamma[:, None, None] * jnp.maximum(diff, 0.0)[None]), 0.0).astype(dt)
    k_decay = jnp.exp(log_gamma[None, :] * (C - 1 - pos)[:, None]).astype(dt)
    q_decay = jnp.exp(log_gamma[None, :] * (pos + 1)[:, None]).astype(dt)
    chunk_decay = jnp.exp(log_gamma * C).astype(dt)
    att = jnp.einsum('bnqhd,bnkhd->bnhqk', qc, kc) * decay[None, None]
    o_intra = jnp.einsum('bnhqk,bnkhe->bnqhe', att, vc)
    S = jnp.einsum('bnkhd,kh,bnkhe->nbhde', kc, k_decay, vc)

    def step(R, S_i):
        return chunk_decay[None, :, None, None] * R + S_i, R

    _, R_prev = lax.scan(step, jnp.zeros(S.shape[1:], S.dtype), S)
    o_inter = jnp.einsum('bnqhd,qh,nbhde->bnqhe', qc, q_decay, R_prev)
    return (o_intra + o_inter).reshape(B, L, H, dv)


def bidirectional_retention(q, k, v, gate, decay_fwd, decay_bwd, norm_g):
    B, L, H, dk = q.shape
    q = rotary(q)
    k = rotary(k) * (dk ** -0.5)
    lg_f = jax.nn.log_sigmoid(decay_fwd.astype(jnp.float32))
    lg_b = jax.nn.log_sigmoid(decay_bwd.astype(jnp.float32))
    o_f = chunk_retention(q, k, v, lg_f, False)
    o_b = jnp.flip(chunk_retention(jnp.flip(q, 1), jnp.flip(k, 1), jnp.flip(v, 1), lg_b, True), 1)
    o = (o_f + o_b).astype(jnp.float32)
    mu = jnp.mean(o, axis=-1, keepdims=True)
    var = jnp.mean(jnp.square(o - mu), axis=-1, keepdims=True)
    o = (o - mu) * lax.rsqrt(var + EPS)
    o = o.reshape(B, L, H * v.shape[-1]) * norm_g.astype(jnp.float32)
    return (jax.nn.silu(gate.astype(jnp.float32)) * o).astype(v.dtype)


MM_DTYPE = jnp.bfloat16
LANES = 128
NOT_TOP = 255.0
SEL_TOKENS = 256
MIX_TOKENS = 512
MIX_KEYS = 8
VMEM_LIMIT = 56 * 1024 * 1024


def _dot_nt(a, b):
    return lax.dot_general(a, b, (((1,), (1,)), ((), ())), preferred_element_type=jnp.float32)


def _topk_rank(s):
    n = s.shape[0]
    iota = lax.broadcasted_iota(jnp.int32, s.shape, 0)
    rank = jnp.full(s.shape, NOT_TOP, jnp.float32)
    tops = []
    for r in range(PEER_TOPK):
        m = jnp.max(s, axis=0, keepdims=True)
        idx = jnp.min(jnp.where(s == m, iota, n), axis=0, keepdims=True)
        hit = iota == idx
        rank = jnp.where(hit, float(r), rank)
        s = jnp.where(hit, -jnp.inf, s)
        tops.append(m)
    return rank, tops


def _young_counts(a_top, b_top):
    k = PEER_TOPK
    w = a_top[0].shape[1]
    b16 = jnp.concatenate(b_top, axis=0)
    b8 = jnp.concatenate(b_top[:8], axis=0)
    blocks = [a_top[0] + b16] + [a_top[a] + b8 for a in range(1, 8)]
    blocks.append(jnp.concatenate(a_top[8:], axis=0) + b_top[0])
    cand = jnp.concatenate(blocks, axis=0)
    rows = cand.shape[0]
    rho = lax.broadcasted_iota(jnp.int32, (rows, w), 0)
    mid = rho - 16
    a_id = jnp.where(rho < 16, 0, jnp.where(rho < 72, 1 + (mid >> 3), rho - 64))
    b_id = jnp.where(rho < 16, rho, jnp.where(rho < 72, mid & 7, 0))
    pos = a_id * k + b_id
    cand = jnp.where((a_id + 1) * (b_id + 1) <= k, cand, -jnp.inf)
    big = k * k
    a_iota = lax.broadcasted_iota(jnp.int32, (k, w), 0)
    n = jnp.zeros((k, w), jnp.float32)
    best = []
    for _ in range(k):
        m = jnp.max(cand, axis=0, keepdims=True)
        p = jnp.min(jnp.where(cand == m, pos, big), axis=0, keepdims=True)
        cand = jnp.where(pos == p, -jnp.inf, cand)
        n = n + (a_iota == (p >> 4)).astype(jnp.float32)
        best.append(m)
    return n, best


def _peer_select_kernel(x_ref, g_ref, wq_ref, keys_ref, h_ref, r1_ref, e1_ref, n0_ref, c0_ref, hb_ref):
    hd = pl.program_id(1)

    @pl.when(hd == 0)
    def _():
        x = x_ref[...]
        y = x * lax.rsqrt(jnp.mean(x * x, axis=-1, keepdims=True) + EPS) * g_ref[...]
        hb_ref[...] = y.astype(hb_ref.dtype)
        h_ref[...] = y.astype(h_ref.dtype)

    q = jnp.dot(hb_ref[...], wq_ref[...], preferred_element_type=jnp.float32)
    q = q.astype(keys_ref.dtype)
    for c in range(SEL_TOKENS // LANES):
        qc = q[c * LANES:(c + 1) * LANES, :]
        s0 = _dot_nt(keys_ref[0, 0], qc[:, :PEER_HALF])
        s1 = _dot_nt(keys_ref[0, 1], qc[:, PEER_HALF:])
        rank0, a_top = _topk_rank(s0)
        rank1, b_top = _topk_rank(s1)
        n, best = _young_counts(a_top, b_top)
        z = jnp.zeros_like(best[0])
        for m in best:
            z = z + jnp.exp(m - best[0])
        n0 = jnp.zeros_like(s0)
        for a in range(PEER_TOPK):
            n0 = jnp.where(rank0 == float(a), n[a:a + 1, :], n0)
        sl = slice(c * LANES, (c + 1) * LANES)
        r1_ref[:, sl] = rank1
        e1_ref[:, sl] = jnp.where(rank1 < NOT_TOP, jnp.exp(s1 - b_top[0]), 0.0)
        n0_ref[:, sl] = n0
        c0_ref[:, sl] = jnp.exp(s0 - a_top[0]) / z


def _gelu_tanh(x):
    return 0.5 * x * (1.0 + jnp.tanh(math.sqrt(2.0 / math.pi) * (x + 0.044715 * (x * x * x))))


def _peer_mix_kernel(h_ref, u_ref, vt_ref, r1_ref, e1_ref, n0_ref, c0_ref, x_ref, fg_ref, y_ref,
                     acc_ref, a_ref, w_ref):
    e = pl.program_id(1)

    @pl.when(e == 0)
    def _():
        acc_ref[...] = jnp.zeros_like(acc_ref)

    a_ref[...] = _dot_nt(u_ref[...], h_ref[...])

    def per_chunk(c, carry):
        sl = pl.ds(pl.multiple_of(c * LANES, LANES), LANES)
        rows = [pl.ds(pl.multiple_of(hd * N_KEYS + e * MIX_KEYS, MIX_KEYS), MIX_KEYS) for hd in range(PEER_HEADS)]
        n_t = [n0_ref[r, sl] for r in rows]
        c_t = [c0_ref[r, sl] for r in rows]
        for ii in range(MIX_KEYS):
            g = jnp.zeros((N_KEYS, LANES), jnp.float32)
            for hd in range(PEER_HEADS):
                r1 = r1_ref[hd * N_KEYS:(hd + 1) * N_KEYS, sl]
                e1 = e1_ref[hd * N_KEYS:(hd + 1) * N_KEYS, sl]
                g = g + jnp.where(r1 < n_t[hd][ii:ii + 1, :], e1, 0.0) * c_t[hd][ii:ii + 1, :]
            act = _gelu_tanh(a_ref[ii * N_KEYS:(ii + 1) * N_KEYS, sl])
            w_ref[ii * N_KEYS:(ii + 1) * N_KEYS, sl] = (act * g).astype(w_ref.dtype)
        return carry

    lax.fori_loop(0, MIX_TOKENS // LANES, per_chunk, 0)
    acc_ref[...] += jnp.dot(vt_ref[...], w_ref[...], preferred_element_type=jnp.float32)

    @pl.when(e == pl.num_programs(1) - 1)
    def _():
        o = acc_ref[...].T + x_ref[...]
        y_ref[...] = o * lax.rsqrt(jnp.mean(o * o, axis=-1, keepdims=True) + EPS) * fg_ref[...]


def peer_and_final_norm(x, norm2_g, w_q, sub_keys, u, v, final_g):
    T, D = x.shape
    assert T % MIX_TOKENS == 0 and T % SEL_TOKENS == 0 and N_KEYS % MIX_KEYS == 0
    f32 = jnp.float32
    nq = PEER_HEADS * PEER_QDIM
    tab = jax.ShapeDtypeStruct((PEER_HEADS * N_KEYS, T), f32)
    tab_spec = pl.BlockSpec((N_KEYS, SEL_TOKENS), lambda t, h: (h, t))
    h, r1, e1, n0, c0 = pl.pallas_call(
        _peer_select_kernel,
        out_shape=(jax.ShapeDtypeStruct((T, D), MM_DTYPE), tab, tab, tab, tab),
        grid=(T // SEL_TOKENS, PEER_HEADS),
        in_specs=[pl.BlockSpec((SEL_TOKENS, D), lambda t, h: (t, 0)),
                  pl.BlockSpec((1, D), lambda t, h: (0, 0)),
                  pl.BlockSpec((D, PEER_QDIM), lambda t, h: (0, h)),
                  pl.BlockSpec((1, 2, N_KEYS, PEER_HALF), lambda t, h: (h, 0, 0, 0))],
        out_specs=(pl.BlockSpec((SEL_TOKENS, D), lambda t, h: (t, 0)), tab_spec, tab_spec, tab_spec, tab_spec),
        scratch_shapes=[pltpu.VMEM((SEL_TOKENS, D), MM_DTYPE)],
        compiler_params=pltpu.CompilerParams(dimension_semantics=("parallel", "arbitrary"),
                                             vmem_limit_bytes=VMEM_LIMIT),
        name="peer_select",
    )(x, norm2_g.reshape(1, D), w_q.astype(MM_DTYPE), sub_keys.astype(MM_DTYPE))

    eb = MIX_KEYS * N_KEYS
    mix_tab = pl.BlockSpec((PEER_HEADS * N_KEYS, MIX_TOKENS), lambda t, e: (0, t))
    tok_blk = pl.BlockSpec((MIX_TOKENS, D), lambda t, e: (t, 0))
    return pl.pallas_call(
        _peer_mix_kernel,
        out_shape=jax.ShapeDtypeStruct((T, D), f32),
        grid=(T // MIX_TOKENS, N_KEYS // MIX_KEYS),
        in_specs=[tok_blk,
                  pl.BlockSpec((eb, D), lambda t, e: (e, 0)),
                  pl.BlockSpec((D, eb), lambda t, e: (0, e)),
                  mix_tab, mix_tab, mix_tab, mix_tab,
                  tok_blk,
                  pl.BlockSpec((1, D), lambda t, e: (0, 0))],
        out_specs=tok_blk,
        scratch_shapes=[pltpu.VMEM((D, MIX_TOKENS), f32),
                        pltpu.VMEM((eb, MIX_TOKENS), f32),
                        pltpu.VMEM((eb, MIX_TOKENS), MM_DTYPE)],
        compiler_params=pltpu.CompilerParams(dimension_semantics=("parallel", "arbitrary"),
                                             vmem_limit_bytes=VMEM_LIMIT),
        name="peer_mix",
    )(h, u.astype(MM_DTYPE), v.T.astype(MM_DTYPE), r1, e1, n0, c0, x, final_g.reshape(1, D))


def encoder_layer(x, norm1_g, w_in, na_rpb, ret_decay_fwd, ret_decay_bwd, ret_norm_g,
                  w_out, norm2_g, peer_w_q, peer_sub_keys, peer_u, peer_v, final_g):
    B, L, D = x.shape
    h = rmsnorm(x, norm1_g)
    proj = h @ w_in
    cuts = [NA_WIDTH, 2 * NA_WIDTH, 3 * NA_WIDTH,
            3 * NA_WIDTH + RET_WIDTH, 3 * NA_WIDTH + 2 * RET_WIDTH, 3 * NA_WIDTH + 3 * RET_WIDTH]
    na_q, na_k, na_v, r_q, r_k, r_v, r_g = jnp.split(proj, cuts, axis=-1)
    na_shape = (B, L, NA_HEADS, NA_HEAD_DIM)
    na_o = neighbourhood_attention(na_q.reshape(na_shape), na_k.reshape(na_shape),
                                   na_v.reshape(na_shape), na_rpb)
    r_shape = (B, L, RET_HEADS, RET_HEAD_DIM)
    ret_o = bidirectional_retention(r_q.reshape(r_shape), r_k.reshape(r_shape), r_v.reshape(r_shape),
                                    r_g, ret_decay_fwd, ret_decay_bwd, ret_norm_g)
    x = x + jnp.concatenate([na_o, ret_o], axis=-1) @ w_out
    y = peer_and_final_norm(x.reshape(B * L, D), norm2_g, peer_w_q, peer_sub_keys, peer_u, peer_v, final_g)
    return y.reshape(B, L, D)


def kernel(x_prompt, x_sample, norm1_g, w_in, na_rpb, ret_decay_fwd, ret_decay_bwd, ret_norm_g,
           w_out, norm2_g, peer_w_q, peer_sub_keys, peer_u, peer_v, final_g):
    assert DEPTH == 1

    def trunk(x):
        return encoder_layer(x, norm1_g[0], w_in[0], na_rpb[0], ret_decay_fwd[0], ret_decay_bwd[0],
                             ret_norm_g[0], w_out[0], norm2_g[0], peer_w_q[0], peer_sub_keys[0],
                             peer_u[0], peer_v[0], final_g)

    return (trunk(x_prompt), trunk(x_sample))
```

```python
import functools
import math
import jax, jax.numpy as jnp
from jax import lax
import numpy as np
from jax.experimental import pallas as pl
from jax.experimental.pallas import tpu as pltpu

D_MODEL = 1024
DEPTH = 1
GRID_W = 64
NA_HEADS = 8
NA_HEAD_DIM = 64
NA_WIDTH = NA_HEADS * NA_HEAD_DIM
NA_ROWS = 8
NA_COLS = 16
RET_HEADS = 4
RET_HEAD_DIM = 128
RET_WIDTH = RET_HEADS * RET_HEAD_DIM
RET_CHUNK = 128
ROPE_BASE = 10000.0
PEER_HEADS = 8
PEER_QDIM = 256
PEER_HALF = PEER_QDIM // 2
N_KEYS = 128
N_EXPERTS = N_KEYS * N_KEYS
PEER_TOPK = 16
PEER_BLOCK = 128
EPS = 1e-6


def rmsnorm(x, g):
    xf = x.astype(jnp.float32)
    y = xf * lax.rsqrt(jnp.mean(xf * xf, axis=-1, keepdims=True) + EPS)
    return (y * g.astype(jnp.float32)).astype(x.dtype)


def rotary(x):
    L, d = x.shape[1], x.shape[-1]
    inv = 1.0 / (ROPE_BASE ** (jnp.arange(0, d, 2, dtype=jnp.float32) / d))
    ang = jnp.arange(L, dtype=jnp.float32)[:, None] * inv[None, :]
    cos = jnp.cos(ang)[None, :, None, :]
    sin = jnp.sin(ang)[None, :, None, :]
    x1 = x[..., : d // 2].astype(jnp.float32)
    x2 = x[..., d // 2:].astype(jnp.float32)
    return jnp.concatenate([x1 * cos - x2 * sin, x1 * sin + x2 * cos], axis=-1).astype(x.dtype)


def neighbourhood_attention(q, k, v, rpb):
    B, L, H, dh = q.shape
    rows = L // GRID_W
    kr = min(NA_ROWS, rows)
    qg = q.reshape(B, rows, GRID_W, H, dh)
    kg = k.reshape(B, rows, GRID_W, H, dh)
    vg = v.reshape(B, rows, GRID_W, H, dh)
    r_idx = jnp.arange(rows)
    row_start = jnp.clip(r_idx - kr // 2, 0, rows - kr)
    c_idx = jnp.arange(GRID_W)
    col_start = jnp.clip(c_idx - NA_COLS // 2, 0, GRID_W - NA_COLS)
    col_idx = col_start[:, None] + jnp.arange(NA_COLS)[None, :]
    col_bias_idx = col_idx - c_idx[:, None] + (NA_COLS - 1)
    scale = dh ** -0.5

    def one_row(args):
        qr, rs, ri = args
        k_band = lax.dynamic_slice_in_dim(kg, rs, kr, axis=1)
        v_band = lax.dynamic_slice_in_dim(vg, rs, kr, axis=1)
        k_nb = k_band[:, :, col_idx]
        v_nb = v_band[:, :, col_idx]
        s = jnp.einsum('bqhd,brqjhd->bhqrj', qr, k_nb).astype(jnp.float32) * scale
        row_bias_idx = rs + jnp.arange(kr) - ri + (NA_ROWS - 1)
        bias = rpb[:, row_bias_idx][:, :, col_bias_idx]
        s = s + jnp.transpose(bias, (0, 2, 1, 3)).astype(jnp.float32)[None]
        p = jax.nn.softmax(s.reshape(B, H, GRID_W, kr * NA_COLS), axis=-1)
        p = p.reshape(B, H, GRID_W, kr, NA_COLS).astype(v.dtype)
        return jnp.einsum('bhqrj,brqjhd->bqhd', p, v_nb)

    out = lax.map(one_row, (jnp.transpose(qg, (1, 0, 2, 3, 4)), row_start, r_idx))
    return jnp.transpose(out, (1, 0, 2, 3, 4)).reshape(B, L, H * dh)


def chunk_retention(q, k, v, log_gamma, strict):
    B, L, H, dk = q.shape
    dv = v.shape[-1]
    C = RET_CHUNK
    nc = L // C
    dt = q.dtype
    qc = q.reshape(B, nc, C, H, dk)
    kc = k.reshape(B, nc, C, H, dk)
    vc = v.reshape(B, nc, C, H, dv)
    pos = jnp.arange(C, dtype=jnp.float32)
    diff = pos[:, None] - pos[None, :]
    mask = (diff > 0) if strict else (diff >= 0)
    decay = jnp.where(mask[None], jnp.exp(log_gamma[:, None, None] * jnp.maximum(diff, 0.0)[None]), 0.0).astype(dt)
    k_decay = jnp.exp(log_gamma[None, :] * (C - 1 - pos)[:, None]).astype(dt)
    q_decay = jnp.exp(log_gamma[None, :] * (pos + 1)[:, None]).astype(dt)
    chunk_decay = jnp.exp(log_gamma * C).astype(dt)
    att = jnp.einsum('bnqhd,bnkhd->bnhqk', qc, kc) * decay[None, None]
    o_intra = jnp.einsum('bnhqk,bnkhe->bnqhe', att, vc)
    S = jnp.einsum('bnkhd,kh,bnkhe->nbhde', kc, k_decay, vc)

    def step(R, S_i):
        return chunk_decay[None, :, None, None] * R + S_i, R

    _, R_prev = lax.scan(step, jnp.zeros(S.shape[1:], S.dtype), S)
    o_inter = jnp.einsum('bnqhd,qh,nbhde->bnqhe', qc, q_decay, R_prev)
    return (o_intra + o_inter).reshape(B, L, H, dv)


def bidirectional_retention(q, k, v, gate, decay_fwd, decay_bwd, norm_g):
    B, L, H, dk = q.shape
    q = rotary(q)
    k = rotary(k) * (dk ** -0.5)
    lg_f = jax.nn.log_sigmoid(decay_fwd.astype(jnp.float32))
    lg_b = jax.nn.log_sigmoid(decay_bwd.astype(jnp.float32))
    o_f = chunk_retention(q, k, v, lg_f, False)
    o_b = jnp.flip(chunk_retention(jnp.flip(q, 1), jnp.flip(k, 1), jnp.flip(v, 1), lg_b, True), 1)
    o = (o_f + o_b).astype(jnp.float32)
    mu = jnp.mean(o, axis=-1, keepdims=True)
    var = jnp.mean(jnp.square(o - mu), axis=-1, keepdims=True)
    o = (o - mu) * lax.rsqrt(var + EPS)
    o = o.reshape(B, L, H * v.shape[-1]) * norm_g.astype(jnp.float32)
    return (jax.nn.silu(gate.astype(jnp.float32)) * o).astype(v.dtype)


MM_DTYPE = jnp.bfloat16
LANES = 128
NOT_TOP = 255.0
SEL_TOKENS = 256
MIX_TOKENS = 512
MIX_KEYS = 8
VMEM_LIMIT = 56 * 1024 * 1024


def _dot_nt(a, b):
    return lax.dot_general(a, b, (((1,), (1,)), ((), ())), preferred_element_type=jnp.float32)


def _topk_rank(s):
    n = s.shape[0]
    iota = lax.broadcasted_iota(jnp.int32, s.shape, 0)
    rank = jnp.full(s.shape, NOT_TOP, jnp.float32)
    tops = []
    for r in range(PEER_TOPK):
        m = jnp.max(s, axis=0, keepdims=True)
        idx = jnp.min(jnp.where(s == m, iota, n), axis=0, keepdims=True)
        hit = iota == idx
        rank = jnp.where(hit, float(r), rank)
        s = jnp.where(hit, -jnp.inf, s)
        tops.append(m)
    return rank, tops


def _young_counts(a_top, b_top):
    k = PEER_TOPK
    w = a_top[0].shape[1]
    b16 = jnp.concatenate(b_top, axis=0)
    b8 = jnp.concatenate(b_top[:8], axis=0)
    blocks = [a_top[0] + b16] + [a_top[a] + b8 for a in range(1, 8)]
    blocks.append(jnp.concatenate(a_top[8:], axis=0) + b_top[0])
    cand = jnp.concatenate(blocks, axis=0)
    rows = cand.shape[0]
    rho = lax.broadcasted_iota(jnp.int32, (rows, w), 0)
    mid = rho - 16
    a_id = jnp.where(rho < 16, 0, jnp.where(rho < 72, 1 + (mid >> 3), rho - 64))
    b_id = jnp.where(rho < 16, rho, jnp.where(rho < 72, mid & 7, 0))
    pos = a_id * k + b_id
    cand = jnp.where((a_id + 1) * (b_id + 1) <= k, cand, -jnp.inf)
    big = k * k
    a_iota = lax.broadcasted_iota(jnp.int32, (k, w), 0)
    n = jnp.zeros((k, w), jnp.float32)
    best = []
    for _ in range(k):
        m = jnp.max(cand, axis=0, keepdims=True)
        p = jnp.min(jnp.where(cand == m, pos, big), axis=0, keepdims=True)
        cand = jnp.where(pos == p, -jnp.inf, cand)
        n = n + (a_iota == (p >> 4)).astype(jnp.float32)
        best.append(m)
    return n, best


def _peer_select_kernel(x_ref, g_ref, wq_ref, keys_ref, h_ref, r1_ref, e1_ref, n0_ref, c0_ref, hb_ref):
    hd = pl.program_id(1)

    @pl.when(hd == 0)
    def _():
        x = x_ref[...]
        y = x * lax.rsqrt(jnp.mean(x * x, axis=-1, keepdims=True) + EPS) * g_ref[...]
        hb_ref[...] = y.astype(hb_ref.dtype)
        h_ref[...] = y.astype(h_ref.dtype)

    q = jnp.dot(hb_ref[...], wq_ref[...], preferred_element_type=jnp.float32)
    q = q.astype(keys_ref.dtype)
    for c in range(SEL_TOKENS // LANES):
        qc = q[c * LANES:(c + 1) * LANES, :]
        s0 = _dot_nt(keys_ref[0, 0], qc[:, :PEER_HALF])
        s1 = _dot_nt(keys_ref[0, 1], qc[:, PEER_HALF:])
        rank0, a_top = _topk_rank(s0)
        rank1, b_top = _topk_rank(s1)
        n, best = _young_counts(a_top, b_top)
        z = jnp.zeros_like(best[0])
        for m in best:
            z = z + jnp.exp(m - best[0])
        n0 = jnp.zeros_like(s0)
        for a in range(PEER_TOPK):
            n0 = jnp.where(rank0 == float(a), n[a:a + 1, :], n0)
        sl = slice(c * LANES, (c + 1) * LANES)
        r1_ref[:, sl] = rank1
        e1_ref[:, sl] = jnp.where(rank1 < NOT_TOP, jnp.exp(s1 - b_top[0]), 0.0)
        n0_ref[:, sl] = n0
        c0_ref[:, sl] = jnp.exp(s0 - a_top[0]) / z


def _gelu_tanh(x):
    return 0.5 * x * (1.0 + jnp.tanh(math.sqrt(2.0 / math.pi) * (x + 0.044715 * (x * x * x))))


def _peer_mix_kernel(h_ref, u_ref, vt_ref, r1_ref, e1_ref, n0_ref, c0_ref, x_ref, fg_ref, y_ref,
                     acc_ref, a_ref, w_ref):
    e = pl.program_id(1)

    @pl.when(e == 0)
    def _():
        acc_ref[...] = jnp.zeros_like(acc_ref)

    a_ref[...] = _dot_nt(u_ref[...], h_ref[...])

    def per_chunk(c, carry):
        sl = pl.ds(pl.multiple_of(c * LANES, LANES), LANES)
        rows = [pl.ds(pl.multiple_of(hd * N_KEYS + e * MIX_KEYS, MIX_KEYS), MIX_KEYS) for hd in range(PEER_HEADS)]
        n_t = [n0_ref[r, sl] for r in rows]
        c_t = [c0_ref[r, sl] for r in rows]
        for ii in range(MIX_KEYS):
            g = jnp.zeros((N_KEYS, LANES), jnp.float32)
            for hd in range(PEER_HEADS):
                r1 = r1_ref[hd * N_KEYS:(hd + 1) * N_KEYS, sl]
                e1 = e1_ref[hd * N_KEYS:(hd + 1) * N_KEYS, sl]
                g = g + jnp.where(r1 < n_t[hd][ii:ii + 1, :], e1, 0.0) * c_t[hd][ii:ii + 1, :]
            act = _gelu_tanh(a_ref[ii * N_KEYS:(ii + 1) * N_KEYS, sl])
            w_ref[ii * N_KEYS:(ii + 1) * N_KEYS, sl] = (act * g).astype(w_ref.dtype)
        return carry

    lax.fori_loop(0, MIX_TOKENS // LANES, per_chunk, 0)
    acc_ref[...] += jnp.dot(vt_ref[...], w_ref[...], preferred_element_type=jnp.float32)

    @pl.when(e == pl.num_programs(1) - 1)
    def _():
        o = acc_ref[...].T + x_ref[...]
        y_ref[...] = o * lax.rsqrt(jnp.mean(o * o, axis=-1, keepdims=True) + EPS) * fg_ref[...]


def peer_and_final_norm(x, norm2_g, w_q, sub_keys, u, v, final_g):
    T, D = x.shape
    assert T % MIX_TOKENS == 0 and T % SEL_TOKENS == 0 and N_KEYS % MIX_KEYS == 0
    f32 = jnp.float32
    nq = PEER_HEADS * PEER_QDIM
    tab = jax.ShapeDtypeStruct((PEER_HEADS * N_KEYS, T), f32)
    tab_spec = pl.BlockSpec((N_KEYS, SEL_TOKENS), lambda t, h: (h, t))
    h, r1, e1, n0, c0 = pl.pallas_call(
        _peer_select_kernel,
        out_shape=(jax.ShapeDtypeStruct((T, D), MM_DTYPE), tab, tab, tab, tab),
        grid=(T // SEL_TOKENS, PEER_HEADS),
        in_specs=[pl.BlockSpec((SEL_TOKENS, D), lambda t, h: (t, 0)),
                  pl.BlockSpec((1, D), lambda t, h: (0, 0)),
                  pl.BlockSpec((D, PEER_QDIM), lambda t, h: (0, h)),
                  pl.BlockSpec((1, 2, N_KEYS, PEER_HALF), lambda t, h: (h, 0, 0, 0))],
        out_specs=(pl.BlockSpec((SEL_TOKENS, D), lambda t, h: (t, 0)), tab_spec, tab_spec, tab_spec, tab_spec),
        scratch_shapes=[pltpu.VMEM((SEL_TOKENS, D), MM_DTYPE)],
        compiler_params=pltpu.CompilerParams(dimension_semantics=("parallel", "arbitrary"),
                                             vmem_limit_bytes=VMEM_LIMIT),
        name="peer_select",
    )(x, norm2_g.reshape(1, D), w_q.astype(MM_DTYPE), sub_keys.astype(MM_DTYPE))

    eb = MIX_KEYS * N_KEYS
    mix_tab = pl.BlockSpec((PEER_HEADS * N_KEYS, MIX_TOKENS), lambda t, e: (0, t))
    tok_blk = pl.BlockSpec((MIX_TOKENS, D), lambda t, e: (t, 0))
    return pl.pallas_call(
        _peer_mix_kernel,
        out_shape=jax.ShapeDtypeStruct((T, D), f32),
        grid=(T // MIX_TOKENS, N_KEYS // MIX_KEYS),
        in_specs=[tok_blk,
                  pl.BlockSpec((eb, D), lambda t, e: (e, 0)),
                  pl.BlockSpec((D, eb), lambda t, e: (0, e)),
                  mix_tab, mix_tab, mix_tab, mix_tab,
                  tok_blk,
                  pl.BlockSpec((1, D), lambda t, e: (0, 0))],
        out_specs=tok_blk,
        scratch_shapes=[pltpu.VMEM((D, MIX_TOKENS), f32),
                        pltpu.VMEM((eb, MIX_TOKENS), f32),
                        pltpu.VMEM((eb, MIX_TOKENS), MM_DTYPE)],
        compiler_params=pltpu.CompilerParams(dimension_semantics=("parallel", "arbitrary"),
                                             vmem_limit_bytes=VMEM_LIMIT),
        name="peer_mix",
    )(h, u.astype(MM_DTYPE), v.T.astype(MM_DTYPE), r1, e1, n0, c0, x, final_g.reshape(1, D))


NA_BAND = NA_ROWS * GRID_W
NA_MASKED = -1e30


def _na_bias_table(rpb):
    cq = jnp.arange(GRID_W)[:, None]
    ck = jnp.arange(GRID_W)[None, :]
    cs = jnp.clip(cq - NA_COLS // 2, 0, GRID_W - NA_COLS)
    valid = (ck >= cs) & (ck < cs + NA_COLS)
    cb = jnp.clip(ck - cq + NA_COLS - 1, 0, 2 * NA_COLS - 2)
    rb = jnp.arange(NA_ROWS)[:, None] + jnp.arange(NA_ROWS)[None, :]
    tab = rpb[:, rb][:, :, :, cb]
    tab = jnp.where(valid, tab, NA_MASKED)
    return jnp.transpose(tab, (1, 0, 3, 2, 4)).reshape(NA_ROWS, NA_HEADS, GRID_W, NA_BAND)


def _na_row_start(r, rows):
    return jnp.clip(r - NA_ROWS // 2, 0, rows - NA_ROWS)


def _na_kernel(q_ref, k_ref, v_ref, tab_ref, o_ref, *, rows):
    start = pl.multiple_of(_na_row_start(pl.program_id(1), rows) * GRID_W, GRID_W)
    kb = k_ref[0, pl.ds(start, NA_BAND), :]
    vb = v_ref[0, pl.ds(start, NA_BAND), :]
    q = q_ref[0]
    lane = lax.broadcasted_iota(jnp.int32, (GRID_W, LANES), 1)
    outs = []
    for hp in range(NA_HEADS // 2):
        sl = slice(hp * LANES, (hp + 1) * LANES)
        qp, kp, vp = q[:, sl], kb[:, sl], vb[:, sl]
        o_pair = None
        for s in range(2):
            mine = (lane >= NA_HEAD_DIM) if s else (lane < NA_HEAD_DIM)
            sc = _dot_nt(jnp.where(mine, qp, jnp.zeros_like(qp)), kp) + tab_ref[0, 2 * hp + s]
            p = jnp.exp(sc - jnp.max(sc, axis=-1, keepdims=True))
            o = jnp.dot(p.astype(vp.dtype), vp, preferred_element_type=jnp.float32)
            o = o / jnp.sum(p, axis=-1, keepdims=True)
            o_pair = o if s == 0 else jnp.where(mine, o, o_pair)
        outs.append(o_pair)
    o_ref[0] = jnp.concatenate(outs, axis=-1).astype(o_ref.dtype)


def neighbourhood_attention_pallas(q, k, v, rpb):
    B, L, W = q.shape
    rows = L // GRID_W
    assert rows >= NA_ROWS and W == NA_HEADS * NA_HEAD_DIM and 2 * NA_HEAD_DIM == LANES
    q = (q * NA_HEAD_DIM ** -0.5).astype(MM_DTYPE)
    seq = pl.BlockSpec((1, L, W), lambda b, r: (b, 0, 0))
    row = pl.BlockSpec((1, GRID_W, W), lambda b, r: (b, r, 0))
    tab = pl.BlockSpec((1, NA_HEADS, GRID_W, NA_BAND),
                       lambda b, r: (_na_row_start(r, rows) - r + NA_ROWS - 1, 0, 0, 0))
    return pl.pallas_call(
        functools.partial(_na_kernel, rows=rows),
        out_shape=jax.ShapeDtypeStruct((B, L, W), jnp.float32),
        grid=(B, rows),
        in_specs=[row, seq, seq, tab],
        out_specs=row,
        compiler_params=pltpu.CompilerParams(dimension_semantics=("parallel", "arbitrary"),
                                             vmem_limit_bytes=VMEM_LIMIT),
        name="na_attention",
    )(q, k.astype(MM_DTYPE), v.astype(MM_DTYPE), _na_bias_table(rpb))


SUBLANES = 8


def _rotate_half(x, cos, sin_signed):
    return x * cos + pltpu.roll(x, x.shape[-1] // 2, axis=1) * sin_signed


def _dot_tn(a, b):
    return lax.dot_general(a, b, (((0,), (0,)), ((), ())), preferred_element_type=jnp.float32)


def _ret_kernel(q_ref, k_ref, v_ref, g_ref, cos_ref, sin_ref, df_ref, db_ref, ng_ref, o_ref,
                of_ref, ob_ref, *, nc):
    C = RET_CHUNK
    mm = o_ref.dtype
    lg_f = jax.nn.log_sigmoid(df_ref[0])[0:1, :]
    lg_b = jax.nn.log_sigmoid(db_ref[0])[0:1, :]
    pos = lax.broadcasted_iota(jnp.int32, (C, C), 0).astype(jnp.float32)
    diff = pos - lax.broadcasted_iota(jnp.int32, (C, C), 1).astype(jnp.float32)
    dec_f = jnp.where(diff >= 0, jnp.exp(lg_f * jnp.maximum(diff, 0.0)), 0.0)
    dec_b = jnp.where(diff < 0, jnp.exp(lg_b * jnp.maximum(-diff, 0.0)), 0.0)
    kdec_f = jnp.exp(lg_f * (C - 1 - pos))
    qdec_f = jnp.exp(lg_f * (pos + 1))
    kdec_b = jnp.exp(lg_b * pos)
    qdec_b = jnp.exp(lg_b * (C - pos))
    step_f = jnp.exp(lg_f * C)
    step_b = jnp.exp(lg_b * C)
    scale = RET_HEAD_DIM ** -0.5

    def one_direction(c, state, att_decay, qdec, kdec, step, out_ref):
        sl = pl.ds(pl.multiple_of(c * C, C), C)
        q = _rotate_half(q_ref[0, sl, :], cos_ref[sl, :], sin_ref[sl, :])
        k = _rotate_half(k_ref[0, sl, :], cos_ref[sl, :], sin_ref[sl, :]) * scale
        v = v_ref[0, sl, :].astype(mm)
        att = _dot_nt(q.astype(mm), k.astype(mm)) * att_decay
        o = jnp.dot(att.astype(mm), v, preferred_element_type=jnp.float32)
        o = o + jnp.dot((q * qdec).astype(mm), state.astype(mm), preferred_element_type=jnp.float32)
        out_ref[sl, :] = o
        return step * state + _dot_tn((k * kdec).astype(mm), v)

    def body(c, carry):
        rf, rb = carry
        rf = one_direction(c, rf, dec_f, qdec_f, kdec_f, step_f, of_ref)
        rb = one_direction(nc - 1 - c, rb, dec_b, qdec_b, kdec_b, step_b, ob_ref)
        return rf, rb

    zero = jnp.zeros((RET_HEAD_DIM, RET_HEAD_DIM), jnp.float32)
    lax.fori_loop(0, nc, body, (zero, zero))

    def finish(c, carry):
        sl = pl.ds(pl.multiple_of(c * C, C), C)
        o = of_ref[sl, :] + ob_ref[sl, :]
        mu = jnp.mean(o, axis=-1, keepdims=True)
        var = jnp.mean(jnp.square(o - mu), axis=-1, keepdims=True)
        o = (o - mu) * lax.rsqrt(var + EPS) * ng_ref[...]
        o_ref[0, sl, :] = (jax.nn.silu(g_ref[0, sl, :]) * o).astype(mm)
        return carry

    lax.fori_loop(0, nc, finish, 0)


def retention_pallas(q, k, v, gate, decay_fwd, decay_bwd, norm_g):
    B, L, W = q.shape
    dh = RET_HEAD_DIM
    assert W == RET_HEADS * dh and dh == LANES and L % RET_CHUNK == 0
    inv = 1.0 / (ROPE_BASE ** (jnp.arange(0, dh, 2, dtype=jnp.float32) / dh))
    ang = jnp.arange(L, dtype=jnp.float32)[:, None] * inv[None, :]
    cos = jnp.concatenate([jnp.cos(ang), jnp.cos(ang)], axis=-1)
    sin = jnp.concatenate([-jnp.sin(ang), jnp.sin(ang)], axis=-1)
    bcast = lambda d: jnp.broadcast_to(d.astype(jnp.float32)[:, None, None], (RET_HEADS, SUBLANES, dh))
    seq = pl.BlockSpec((1, L, dh), lambda b, h: (b, 0, h))
    tab = pl.BlockSpec((L, dh), lambda b, h: (0, 0))
    dec = pl.BlockSpec((1, SUBLANES, dh), lambda b, h: (h, 0, 0))
    return pl.pallas_call(
        functools.partial(_ret_kernel, nc=L // RET_CHUNK),
        out_shape=jax.ShapeDtypeStruct((B, L, W), MM_DTYPE),
        grid=(B, RET_HEADS),
        in_specs=[seq, seq, seq, seq, tab, tab, dec, dec, pl.BlockSpec((1, dh), lambda b, h: (0, h))],
        out_specs=seq,
        scratch_shapes=[pltpu.VMEM((L, dh), jnp.float32), pltpu.VMEM((L, dh), jnp.float32)],
        compiler_params=pltpu.CompilerParams(dimension_semantics=("parallel", "parallel"),
                                             vmem_limit_bytes=VMEM_LIMIT),
        name="retention",
    )(q, k, v, gate, cos, sin, bcast(decay_fwd), bcast(decay_bwd), norm_g.reshape(1, W))


PROJ_TOKENS = 512
N_NA_STREAMS = 3
N_RET_STREAMS = 4


def _in_proj_kernel(x_ref, g_ref, w_ref, *out_refs):
    x = x_ref[...]
    h = (x * lax.rsqrt(jnp.mean(x * x, axis=-1, keepdims=True) + EPS) * g_ref[...]).astype(w_ref.dtype)
    for j, o_ref in enumerate(out_refs):
        o_ref[...] = jnp.dot(h, w_ref[:, j * NA_WIDTH:(j + 1) * NA_WIDTH],
                             preferred_element_type=jnp.float32).astype(o_ref.dtype)


def in_projection(x, norm_g, w_in):
    T, D = x.shape
    assert NA_WIDTH == RET_WIDTH and T % PROJ_TOKENS == 0
    n = N_NA_STREAMS + N_RET_STREAMS
    assert w_in.shape == (D, n * NA_WIDTH)
    tok = pl.BlockSpec((PROJ_TOKENS, D), lambda t: (t, 0))
    out = pl.BlockSpec((PROJ_TOKENS, NA_WIDTH), lambda t: (t, 0))
    return pl.pallas_call(
        _in_proj_kernel,
        out_shape=tuple(jax.ShapeDtypeStruct((T, NA_WIDTH), jnp.float32) for _ in range(n)),
        grid=(T // PROJ_TOKENS,),
        in_specs=[tok, pl.BlockSpec((1, D), lambda t: (0, 0)), pl.BlockSpec(w_in.shape, lambda t: (0, 0))],
        out_specs=tuple(out for _ in range(n)),
        compiler_params=pltpu.CompilerParams(dimension_semantics=("parallel",), vmem_limit_bytes=VMEM_LIMIT),
        name="in_projection",
    )(x, norm_g.reshape(1, D), w_in.astype(MM_DTYPE))


def _out_proj_kernel(a_ref, r_ref, w_ref, x_ref, o_ref):
    o = jnp.dot(a_ref[...].astype(w_ref.dtype), w_ref[:NA_WIDTH, :], preferred_element_type=jnp.float32)
    o = o + jnp.dot(r_ref[...].astype(w_ref.dtype), w_ref[NA_WIDTH:, :], preferred_element_type=jnp.float32)
    o_ref[...] = x_ref[...] + o


def out_projection(na_o, ret_o, w_out, x):
    T, D = x.shape
    tok = pl.BlockSpec((PROJ_TOKENS, D), lambda t: (t, 0))
    half = pl.BlockSpec((PROJ_TOKENS, NA_WIDTH), lambda t: (t, 0))
    return pl.pallas_call(
        _out_proj_kernel,
        out_shape=jax.ShapeDtypeStruct((T, D), jnp.float32),
        grid=(T // PROJ_TOKENS,),
        in_specs=[half, half, pl.BlockSpec(w_out.shape, lambda t: (0, 0)), tok],
        out_specs=tok,
        compiler_params=pltpu.CompilerParams(dimension_semantics=("parallel",), vmem_limit_bytes=VMEM_LIMIT),
        name="out_projection",
    )(na_o, ret_o, w_out.astype(MM_DTYPE), x)


def encoder_layer(x, norm1_g, w_in, na_rpb, ret_decay_fwd, ret_decay_bwd, ret_norm_g,
                  w_out, norm2_g, peer_w_q, peer_sub_keys, peer_u, peer_v, final_g):
    B, L, D = x.shape
    xt = x.reshape(B * L, D)
    streams = [s.reshape(B, L, NA_WIDTH) for s in in_projection(xt, norm1_g, w_in)]
    na_o = neighbourhood_attention_pallas(*streams[:N_NA_STREAMS], na_rpb)
    ret_o = retention_pallas(*streams[N_NA_STREAMS:], ret_decay_fwd, ret_decay_bwd, ret_norm_g)
    x1 = out_projection(na_o.reshape(B * L, NA_WIDTH), ret_o.reshape(B * L, RET_WIDTH), w_out, xt)
    y = peer_and_final_norm(x1, norm2_g, peer_w_q, peer_sub_keys, peer_u, peer_v, final_g)
    return y.reshape(B, L, D)


def kernel(x_prompt, x_sample, norm1_g, w_in, na_rpb, ret_decay_fwd, ret_decay_bwd, ret_norm_g,
           w_out, norm2_g, peer_w_q, peer_sub_keys, peer_u, peer_v, final_g):
    assert DEPTH == 1

    def trunk(x):
        return encoder_layer(x, norm1_g[0], w_in[0], na_rpb[0], ret_decay_fwd[0], ret_decay_bwd[0],
                             ret_norm_g[0], w_out[0], norm2_g[0], peer_w_q[0], peer_sub_keys[0],
                             peer_u[0], peer_v[0], final_g)

    return (trunk(x_prompt), trunk(x_sample))
```

```python
import functools
import math
import jax, jax.numpy as jnp
from jax import lax
import numpy as np
from jax.experimental import pallas as pl
from jax.experimental.pallas import tpu as pltpu

D_MODEL = 1024
DEPTH = 1
GRID_W = 64
NA_HEADS = 8
NA_HEAD_DIM = 64
NA_WIDTH = NA_HEADS * NA_HEAD_DIM
NA_ROWS = 8
NA_COLS = 16
RET_HEADS = 4
RET_HEAD_DIM = 128
RET_WIDTH = RET_HEADS * RET_HEAD_DIM
RET_CHUNK = 128
ROPE_BASE = 10000.0
PEER_HEADS = 8
PEER_QDIM = 256
PEER_HALF = PEER_QDIM // 2
N_KEYS = 128
N_EXPERTS = N_KEYS * N_KEYS
PEER_TOPK = 16
PEER_BLOCK = 128
EPS = 1e-6


def rmsnorm(x, g):
    xf = x.astype(jnp.float32)
    y = xf * lax.rsqrt(jnp.mean(xf * xf, axis=-1, keepdims=True) + EPS)
    return (y * g.astype(jnp.float32)).astype(x.dtype)


def rotary(x):
    L, d = x.shape[1], x.shape[-1]
    inv = 1.0 / (ROPE_BASE ** (jnp.arange(0, d, 2, dtype=jnp.float32) / d))
    ang = jnp.arange(L, dtype=jnp.float32)[:, None] * inv[None, :]
    cos = jnp.cos(ang)[None, :, None, :]
    sin = jnp.sin(ang)[None, :, None, :]
    x1 = x[..., : d // 2].astype(jnp.float32)
    x2 = x[..., d // 2:].astype(jnp.float32)
    return jnp.concatenate([x1 * cos - x2 * sin, x1 * sin + x2 * cos], axis=-1).astype(x.dtype)


def neighbourhood_attention(q, k, v, rpb):
    B, L, H, dh = q.shape
    rows = L // GRID_W
    kr = min(NA_ROWS, rows)
    qg = q.reshape(B, rows, GRID_W, H, dh)
    kg = k.reshape(B, rows, GRID_W, H, dh)
    vg = v.reshape(B, rows, GRID_W, H, dh)
    r_idx = jnp.arange(rows)
    row_start = jnp.clip(r_idx - kr // 2, 0, rows - kr)
    c_idx = jnp.arange(GRID_W)
    col_start = jnp.clip(c_idx - NA_COLS // 2, 0, GRID_W - NA_COLS)
    col_idx = col_start[:, None] + jnp.arange(NA_COLS)[None, :]
    col_bias_idx = col_idx - c_idx[:, None] + (NA_COLS - 1)
    scale = dh ** -0.5

    def one_row(args):
        qr, rs, ri = args
        k_band = lax.dynamic_slice_in_dim(kg, rs, kr, axis=1)
        v_band = lax.dynamic_slice_in_dim(vg, rs, kr, axis=1)
        k_nb = k_band[:, :, col_idx]
        v_nb = v_band[:, :, col_idx]
        s = jnp.einsum('bqhd,brqjhd->bhqrj', qr, k_nb).astype(jnp.float32) * scale
        row_bias_idx = rs + jnp.arange(kr) - ri + (NA_ROWS - 1)
        bias = rpb[:, row_bias_idx][:, :, col_bias_idx]
        s = s + jnp.transpose(bias, (0, 2, 1, 3)).astype(jnp.float32)[None]
        p = jax.nn.softmax(s.reshape(B, H, GRID_W, kr * NA_COLS), axis=-1)
        p = p.reshape(B, H, GRID_W, kr, NA_COLS).astype(v.dtype)
        return jnp.einsum('bhqrj,brqjhd->bqhd', p, v_nb)

    out = lax.map(one_row, (jnp.transpose(qg, (1, 0, 2, 3, 4)), row_start, r_idx))
    return jnp.transpose(out, (1, 0, 2, 3, 4)).reshape(B, L, H * dh)


def chunk_retention(q, k, v, log_gamma, strict):
    B, L, H, dk = q.shape
    dv = v.shape[-1]
    C = RET_CHUNK
    nc = L // C
    dt = q.dtype
    qc = q.reshape(B, nc, C, H, dk)
    kc = k.reshape(B, nc, C, H, dk)
    vc = v.reshape(B, nc, C, H, dv)
    pos = jnp.arange(C, dtype=jnp.float32)
    diff = pos[:, None] - pos[None, :]
    mask = (diff > 0) if strict else (diff >= 0)
    decay = jnp.where(mask[None], jnp.exp(log_gamma[:, None, None] * jnp.maximum(diff, 0.0)[None]), 0.0).astype(dt)
    k_decay = jnp.exp(log_gamma[None, :] * (C - 1 - pos)[:, None]).astype(dt)
    q_decay = jnp.exp(log_gamma[None, :] * (pos + 1)[:, None]).astype(dt)
    chunk_decay = jnp.exp(log_gamma * C).astype(dt)
    att = jnp.einsum('bnqhd,bnkhd->bnhqk', qc, kc) * decay[None, None]
    o_intra = jnp.einsum('bnhqk,bnkhe->bnqhe', att, vc)
    S = jnp.einsum('bnkhd,kh,bnkhe->nbhde', kc, k_decay, vc)

    def step(R, S_i):
        return chunk_decay[None, :, None, None] * R + S_i, R

    _, R_prev = lax.scan(step, jnp.zeros(S.shape[1:], S.dtype), S)
    o_inter = jnp.einsum('bnqhd,qh,nbhde->bnqhe', qc, q_decay, R_prev)
    return (o_intra + o_inter).reshape(B, L, H, dv)


def bidirectional_retention(q, k, v, gate, decay_fwd, decay_bwd, norm_g):
    B, L, H, dk = q.shape
    q = rotary(q)
    k = rotary(k) * (dk ** -0.5)
    lg_f = jax.nn.log_sigmoid(decay_fwd.astype(jnp.float32))
    lg_b = jax.nn.log_sigmoid(decay_bwd.astype(jnp.float32))
    o_f = chunk_retention(q, k, v, lg_f, False)
    o_b = jnp.flip(chunk_retention(jnp.flip(q, 1), jnp.flip(k, 1), jnp.flip(v, 1), lg_b, True), 1)
    o = (o_f + o_b).astype(jnp.float32)
    mu = jnp.mean(o, axis=-1, keepdims=True)
    var = jnp.mean(jnp.square(o - mu), axis=-1, keepdims=True)
    o = (o - mu) * lax.rsqrt(var + EPS)
    o = o.reshape(B, L, H * v.shape[-1]) * norm_g.astype(jnp.float32)
    return (jax.nn.silu(gate.astype(jnp.float32)) * o).astype(v.dtype)


MM_DTYPE = jnp.bfloat16
GATE_DTYPE = jnp.bfloat16
GATE_PACK = 2
LANES = 128
NOT_TOP = 255.0
SEL_TOKENS = 256
MIX_TOKENS = 512
MIX_KEYS = 8
VMEM_LIMIT = 56 * 1024 * 1024


def _dot_nt(a, b):
    return lax.dot_general(a, b, (((1,), (1,)), ((), ())), preferred_element_type=jnp.float32)


def _topk_rank(s):
    iota = lax.broadcasted_iota(jnp.int32, s.shape, 0).astype(jnp.float32)
    rank = jnp.full(s.shape, NOT_TOP, jnp.float32)
    tops = []
    for r in range(PEER_TOPK):
        m = jnp.max(s, axis=0, keepdims=True)
        idx = jnp.min(jnp.where(s == m, iota, float(s.shape[0])), axis=0, keepdims=True)
        hit = iota == idx
        rank = jnp.where(hit, float(r), rank)
        s = jnp.where(hit, -jnp.inf, s)
        tops.append(m)
    return rank, tops


def _young_counts(a_top, b_top):
    k = PEER_TOPK
    w = a_top[0].shape[1]
    b16 = jnp.concatenate(b_top, axis=0)
    b8 = jnp.concatenate(b_top[:8], axis=0)
    blocks = [a_top[0] + b16] + [a_top[a] + b8 for a in range(1, 8)]
    blocks.append(jnp.concatenate(a_top[8:], axis=0) + b_top[0])
    cand = jnp.concatenate(blocks, axis=0)
    rows = cand.shape[0]
    rho = lax.broadcasted_iota(jnp.int32, (rows, w), 0)
    mid = rho - 16
    a_id = jnp.where(rho < 16, 0, jnp.where(rho < 72, 1 + (mid >> 3), rho - 64))
    b_id = jnp.where(rho < 16, rho, jnp.where(rho < 72, mid & 7, 0))
    pos = (a_id * k + b_id).astype(jnp.float32)
    cand = jnp.where((a_id + 1) * (b_id + 1) <= k, cand, -jnp.inf)
    a_iota = lax.broadcasted_iota(jnp.int32, (k, w), 0).astype(jnp.float32)
    n = jnp.zeros((k, w), jnp.float32)
    best = []
    for _ in range(k):
        m = jnp.max(cand, axis=0, keepdims=True)
        p = jnp.min(jnp.where(cand == m, pos, float(k * k)), axis=0, keepdims=True)
        cand = jnp.where(pos == p, -jnp.inf, cand)
        n = n + (a_iota == jnp.floor(p * (1.0 / k))).astype(jnp.float32)
        best.append(m)
    return n, best


def _peer_select_kernel(x_ref, g_ref, wq_ref, keys_ref, h_ref, r1_ref, e1_ref, n0_ref, c0_ref, hb_ref):
    hd = pl.program_id(1)

    @pl.when(hd == 0)
    def _():
        x = x_ref[...]
        y = x * lax.rsqrt(jnp.mean(x * x, axis=-1, keepdims=True) + EPS) * g_ref[...]
        hb_ref[...] = y.astype(hb_ref.dtype)
        h_ref[...] = y.astype(h_ref.dtype)

    q = jnp.dot(hb_ref[...], wq_ref[...], preferred_element_type=jnp.float32)
    q = q.astype(keys_ref.dtype)
    for c in range(SEL_TOKENS // LANES):
        qc = q[c * LANES:(c + 1) * LANES, :]
        s0 = _dot_nt(keys_ref[0, 0], qc[:, :PEER_HALF])
        s1 = _dot_nt(keys_ref[0, 1], qc[:, PEER_HALF:])
        rank0, a_top = _topk_rank(s0)
        rank1, b_top = _topk_rank(s1)
        n, best = _young_counts(a_top, b_top)
        z = jnp.zeros_like(best[0])
        for m in best:
            z = z + jnp.exp(m - best[0])
        n0 = jnp.zeros_like(s0)
        for a in range(PEER_TOPK):
            n0 = jnp.where(rank0 == float(a), n[a:a + 1, :], n0)
        sl = slice(c * LANES, (c + 1) * LANES)
        e1 = jnp.where(rank1 < NOT_TOP, jnp.exp(s1 - b_top[0]), 0.0)
        r1_ref[:, sl] = pltpu.bitcast(rank1.astype(GATE_DTYPE), r1_ref.dtype)
        e1_ref[:, sl] = pltpu.bitcast(e1.astype(GATE_DTYPE), e1_ref.dtype)
        n0_ref[:, sl] = n0
        c0_ref[:, sl] = jnp.exp(s0 - a_top[0]) * (0.5 / z)


def _twice_gelu_tanh(x):
    c = math.sqrt(2.0 / math.pi)
    return x * (1.0 + jnp.tanh(x * (c + (c * 0.044715) * (x * x))))


def _peer_mix_kernel(h_ref, u_ref, vt_ref, r1_ref, e1_ref, n0_ref, c0_ref, x_ref, fg_ref, y_ref,
                     acc_ref, a_ref, w_ref):
    e = pl.program_id(1)

    @pl.when(e == 0)
    def _():
        acc_ref[...] = jnp.zeros_like(acc_ref)

    gd = GATE_DTYPE
    pack = SUBLANES * GATE_PACK
    krows = N_KEYS // GATE_PACK
    half = MIX_TOKENS // 2
    for hf in range(2):
        a_ref[:, hf * half:(hf + 1) * half] = _dot_nt(u_ref[...], h_ref[hf * half:(hf + 1) * half, :])
    rows = [pl.ds(pl.multiple_of(hd * N_KEYS + e * MIX_KEYS, MIX_KEYS), MIX_KEYS) for hd in range(PEER_HEADS)]
    for hf in range(2):
        for c in range(hf * half // LANES, (hf + 1) * half // LANES):
            sl = slice(c * LANES, (c + 1) * LANES)
            n_t = [n0_ref[r, sl] for r in rows]
            c_t = [c0_ref[r, sl] for r in rows]
            for ii in range(MIX_KEYS):
                g = jnp.zeros((N_KEYS // pack, pack, LANES), gd)
                for hd in range(PEER_HEADS):
                    n_b = jnp.broadcast_to(n_t[hd][ii:ii + 1, :], (pack, LANES)).astype(gd)[None]
                    c_b = jnp.broadcast_to(c_t[hd][ii:ii + 1, :], (pack, LANES)).astype(gd)[None]
                    r1 = pltpu.bitcast(r1_ref[hd * krows:(hd + 1) * krows, sl], gd).reshape(g.shape)
                    e1 = pltpu.bitcast(e1_ref[hd * krows:(hd + 1) * krows, sl], gd).reshape(g.shape)
                    g = g + jnp.where(r1 < n_b, e1, jnp.zeros_like(e1)) * c_b
                act = _twice_gelu_tanh(a_ref[ii * N_KEYS:(ii + 1) * N_KEYS, sl])
                gate = g.reshape(N_KEYS, LANES).astype(jnp.float32)
                w_ref[ii * N_KEYS:(ii + 1) * N_KEYS, sl] = (act * gate).astype(w_ref.dtype)
        hs = slice(hf * half, (hf + 1) * half)
        acc_ref[:, hs] += jnp.dot(vt_ref[...], w_ref[:, hs], preferred_element_type=jnp.float32)

    @pl.when(e == pl.num_programs(1) - 1)
    def _():
        o = acc_ref[...].T + x_ref[...]
        y_ref[...] = o * lax.rsqrt(jnp.mean(o * o, axis=-1, keepdims=True) + EPS) * fg_ref[...]


def peer_and_final_norm(x, norm2_g, w_q, sub_keys, u, v, final_g):
    T, D = x.shape
    assert T % MIX_TOKENS == 0 and T % SEL_TOKENS == 0 and N_KEYS % MIX_KEYS == 0
    f32 = jnp.float32
    nq = PEER_HEADS * PEER_QDIM
    tab = jax.ShapeDtypeStruct((PEER_HEADS * N_KEYS, T), f32)
    gtab = jax.ShapeDtypeStruct((PEER_HEADS * N_KEYS // GATE_PACK, T), jnp.uint32)
    tab_spec = pl.BlockSpec((N_KEYS, SEL_TOKENS), lambda t, h: (h, t))
    gtab_spec = pl.BlockSpec((N_KEYS // GATE_PACK, SEL_TOKENS), lambda t, h: (h, t))
    h, r1, e1, n0, c0 = pl.pallas_call(
        _peer_select_kernel,
        out_shape=(jax.ShapeDtypeStruct((T, D), MM_DTYPE), gtab, gtab, tab, tab),
        grid=(T // SEL_TOKENS, PEER_HEADS),
        in_specs=[pl.BlockSpec((SEL_TOKENS, D), lambda t, h: (t, 0)),
                  pl.BlockSpec((1, D), lambda t, h: (0, 0)),
                  pl.BlockSpec((D, PEER_QDIM), lambda t, h: (0, h)),
                  pl.BlockSpec((1, 2, N_KEYS, PEER_HALF), lambda t, h: (h, 0, 0, 0))],
        out_specs=(pl.BlockSpec((SEL_TOKENS, D), lambda t, h: (t, 0)), gtab_spec, gtab_spec, tab_spec, tab_spec),
        scratch_shapes=[pltpu.VMEM((SEL_TOKENS, D), MM_DTYPE)],
        compiler_params=pltpu.CompilerParams(dimension_semantics=("parallel", "arbitrary"),
                                             vmem_limit_bytes=VMEM_LIMIT),
        name="peer_select",
    )(x, norm2_g.reshape(1, D), w_q.astype(MM_DTYPE), sub_keys.astype(MM_DTYPE))

    eb = MIX_KEYS * N_KEYS
    mix_tab = pl.BlockSpec((PEER_HEADS * N_KEYS, MIX_TOKENS), lambda t, e: (0, t))
    mix_gtab = pl.BlockSpec((PEER_HEADS * N_KEYS // GATE_PACK, MIX_TOKENS), lambda t, e: (0, t))
    tok_blk = pl.BlockSpec((MIX_TOKENS, D), lambda t, e: (t, 0))
    return pl.pallas_call(
        _peer_mix_kernel,
        out_shape=jax.ShapeDtypeStruct((T, D), f32),
        grid=(T // MIX_TOKENS, N_KEYS // MIX_KEYS),
        in_specs=[tok_blk,
                  pl.BlockSpec((eb, D), lambda t, e: (e, 0)),
                  pl.BlockSpec((D, eb), lambda t, e: (0, e)),
                  mix_gtab, mix_gtab, mix_tab, mix_tab,
                  tok_blk,
                  pl.BlockSpec((1, D), lambda t, e: (0, 0))],
        out_specs=tok_blk,
        scratch_shapes=[pltpu.VMEM((D, MIX_TOKENS), f32),
                        pltpu.VMEM((eb, MIX_TOKENS), f32),
                        pltpu.VMEM((eb, MIX_TOKENS), MM_DTYPE)],
        compiler_params=pltpu.CompilerParams(dimension_semantics=("parallel", "arbitrary"),
                                             vmem_limit_bytes=VMEM_LIMIT),
        name="peer_mix",
    )(h, u.astype(MM_DTYPE), v.T.astype(MM_DTYPE), r1, e1, n0, c0, x, final_g.reshape(1, D))


NA_BAND = NA_ROWS * GRID_W
NA_MASKED = -1e30


def _na_bias_table(rpb):
    cq = jnp.arange(GRID_W)[:, None]
    ck = jnp.arange(GRID_W)[None, :]
    cs = jnp.clip(cq - NA_COLS // 2, 0, GRID_W - NA_COLS)
    valid = (ck >= cs) & (ck < cs + NA_COLS)
    cb = jnp.clip(ck - cq + NA_COLS - 1, 0, 2 * NA_COLS - 2)
    rb = jnp.arange(NA_ROWS)[:, None] + jnp.arange(NA_ROWS)[None, :]
    tab = rpb[:, rb][:, :, :, cb]
    tab = jnp.where(valid, tab, NA_MASKED)
    return jnp.transpose(tab, (1, 0, 3, 2, 4)).reshape(NA_ROWS, NA_HEADS, GRID_W, NA_BAND)


def _na_row_start(r, rows):
    return jnp.clip(r - NA_ROWS // 2, 0, rows - NA_ROWS)


def _na_kernel(q_ref, k_ref, v_ref, tab_ref, o_ref, *, rows):
    start = pl.multiple_of(_na_row_start(pl.program_id(1), rows) * GRID_W, GRID_W)
    kb = k_ref[0, pl.ds(start, NA_BAND), :]
    vb = v_ref[0, pl.ds(start, NA_BAND), :]
    q = q_ref[0]
    lane = lax.broadcasted_iota(jnp.int32, (GRID_W, LANES), 1)
    outs = []
    for hp in range(NA_HEADS // 2):
        sl = slice(hp * LANES, (hp + 1) * LANES)
        qp, kp, vp = q[:, sl], kb[:, sl], vb[:, sl]
        o_pair = None
        for s in range(2):
            mine = (lane >= NA_HEAD_DIM) if s else (lane < NA_HEAD_DIM)
            sc = _dot_nt(jnp.where(mine, qp, jnp.zeros_like(qp)), kp) + tab_ref[0, 2 * hp + s]
            p = jnp.exp(sc - jnp.max(sc, axis=-1, keepdims=True))
            o = jnp.dot(p.astype(vp.dtype), vp, preferred_element_type=jnp.float32)
            o = o / jnp.sum(p, axis=-1, keepdims=True)
            o_pair = o if s == 0 else jnp.where(mine, o, o_pair)
        outs.append(o_pair)
    o_ref[0] = jnp.concatenate(outs, axis=-1).astype(o_ref.dtype)


def neighbourhood_attention_pallas(q, k, v, rpb):
    B, L, W = q.shape
    rows = L // GRID_W
    assert rows >= NA_ROWS and W == NA_HEADS * NA_HEAD_DIM and 2 * NA_HEAD_DIM == LANES
    q = (q * NA_HEAD_DIM ** -0.5).astype(MM_DTYPE)
    seq = pl.BlockSpec((1, L, W), lambda b, r: (b, 0, 0))
    row = pl.BlockSpec((1, GRID_W, W), lambda b, r: (b, r, 0))
    tab = pl.BlockSpec((1, NA_HEADS, GRID_W, NA_BAND),
                       lambda b, r: (_na_row_start(r, rows) - r + NA_ROWS - 1, 0, 0, 0))
    return pl.pallas_call(
        functools.partial(_na_kernel, rows=rows),
        out_shape=jax.ShapeDtypeStruct((B, L, W), jnp.float32),
        grid=(B, rows),
        in_specs=[row, seq, seq, tab],
        out_specs=row,
        compiler_params=pltpu.CompilerParams(dimension_semantics=("parallel", "arbitrary"),
                                             vmem_limit_bytes=VMEM_LIMIT),
        name="na_attention",
    )(q, k.astype(MM_DTYPE), v.astype(MM_DTYPE), _na_bias_table(rpb))


SUBLANES = 8


def _rotate_half(x, cos, sin_signed):
    return x * cos + pltpu.roll(x, x.shape[-1] // 2, axis=1) * sin_signed


def _dot_tn(a, b):
    return lax.dot_general(a, b, (((0,), (0,)), ((), ())), preferred_element_type=jnp.float32)


def _ret_kernel(q_ref, k_ref, v_ref, g_ref, cos_ref, sin_ref, df_ref, db_ref, ng_ref, o_ref,
                of_ref, ob_ref, *, nc):
    C = RET_CHUNK
    mm = o_ref.dtype
    lg_f = jax.nn.log_sigmoid(df_ref[0])[0:1, :]
    lg_b = jax.nn.log_sigmoid(db_ref[0])[0:1, :]
    pos = lax.broadcasted_iota(jnp.int32, (C, C), 0).astype(jnp.float32)
    diff = pos - lax.broadcasted_iota(jnp.int32, (C, C), 1).astype(jnp.float32)
    dec_f = jnp.where(diff >= 0, jnp.exp(lg_f * jnp.maximum(diff, 0.0)), 0.0)
    dec_b = jnp.where(diff < 0, jnp.exp(lg_b * jnp.maximum(-diff, 0.0)), 0.0)
    kdec_f = jnp.exp(lg_f * (C - 1 - pos))
    qdec_f = jnp.exp(lg_f * (pos + 1))
    kdec_b = jnp.exp(lg_b * pos)
    qdec_b = jnp.exp(lg_b * (C - pos))
    step_f = jnp.exp(lg_f * C)
    step_b = jnp.exp(lg_b * C)
    scale = RET_HEAD_DIM ** -0.5

    def one_direction(c, state, att_decay, qdec, kdec, step, out_ref):
        sl = pl.ds(pl.multiple_of(c * C, C), C)
        q = _rotate_half(q_ref[0, sl, :], cos_ref[sl, :], sin_ref[sl, :])
        k = _rotate_half(k_ref[0, sl, :], cos_ref[sl, :], sin_ref[sl, :]) * scale
        v = v_ref[0, sl, :].astype(mm)
        att = _dot_nt(q.astype(mm), k.astype(mm)) * att_decay
        o = jnp.dot(att.astype(mm), v, preferred_element_type=jnp.float32)
        o = o + jnp.dot((q * qdec).astype(mm), state.astype(mm), preferred_element_type=jnp.float32)
        out_ref[sl, :] = o
        return step * state + _dot_tn((k * kdec).astype(mm), v)

    def body(c, carry):
        rf, rb = carry
        rf = one_direction(c, rf, dec_f, qdec_f, kdec_f, step_f, of_ref)
        rb = one_direction(nc - 1 - c, rb, dec_b, qdec_b, kdec_b, step_b, ob_ref)
        return rf, rb

    zero = jnp.zeros((RET_HEAD_DIM, RET_HEAD_DIM), jnp.float32)
    lax.fori_loop(0, nc, body, (zero, zero))

    def finish(c, carry):
        sl = pl.ds(pl.multiple_of(c * C, C), C)
        o = of_ref[sl, :] + ob_ref[sl, :]
        mu = jnp.mean(o, axis=-1, keepdims=True)
        var = jnp.mean(jnp.square(o - mu), axis=-1, keepdims=True)
        o = (o - mu) * lax.rsqrt(var + EPS) * ng_ref[...]
        o_ref[0, sl, :] = (jax.nn.silu(g_ref[0, sl, :]) * o).astype(mm)
        return carry

    lax.fori_loop(0, nc, finish, 0)


def retention_pallas(q, k, v, gate, decay_fwd, decay_bwd, norm_g):
    B, L, W = q.shape
    dh = RET_HEAD_DIM
    assert W == RET_HEADS * dh and dh == LANES and L % RET_CHUNK == 0
    inv = 1.0 / (ROPE_BASE ** (jnp.arange(0, dh, 2, dtype=jnp.float32) / dh))
    ang = jnp.arange(L, dtype=jnp.float32)[:, None] * inv[None, :]
    cos = jnp.concatenate([jnp.cos(ang), jnp.cos(ang)], axis=-1)
    sin = jnp.concatenate([-jnp.sin(ang), jnp.sin(ang)], axis=-1)
    bcast = lambda d: jnp.broadcast_to(d.astype(jnp.float32)[:, None, None], (RET_HEADS, SUBLANES, dh))
    seq = pl.BlockSpec((1, L, dh), lambda b, h: (b, 0, h))
    tab = pl.BlockSpec((L, dh), lambda b, h: (0, 0))
    dec = pl.BlockSpec((1, SUBLANES, dh), lambda b, h: (h, 0, 0))
    return pl.pallas_call(
        functools.partial(_ret_kernel, nc=L // RET_CHUNK),
        out_shape=jax.ShapeDtypeStruct((B, L, W), MM_DTYPE),
        grid=(B, RET_HEADS),
        in_specs=[seq, seq, seq, seq, tab, tab, dec, dec, pl.BlockSpec((1, dh), lambda b, h: (0, h))],
        out_specs=seq,
        scratch_shapes=[pltpu.VMEM((L, dh), jnp.float32), pltpu.VMEM((L, dh), jnp.float32)],
        compiler_params=pltpu.CompilerParams(dimension_semantics=("parallel", "parallel"),
                                             vmem_limit_bytes=VMEM_LIMIT),
        name="retention",
    )(q, k, v, gate, cos, sin, bcast(decay_fwd), bcast(decay_bwd), norm_g.reshape(1, W))


PROJ_TOKENS = 512
N_NA_STREAMS = 3
N_RET_STREAMS = 4


def _in_proj_kernel(x_ref, g_ref, w_ref, *out_refs):
    x = x_ref[...]
    h = (x * lax.rsqrt(jnp.mean(x * x, axis=-1, keepdims=True) + EPS) * g_ref[...]).astype(w_ref.dtype)
    for j, o_ref in enumerate(out_refs):
        o_ref[...] = jnp.dot(h, w_ref[:, j * NA_WIDTH:(j + 1) * NA_WIDTH],
                             preferred_element_type=jnp.float32).astype(o_ref.dtype)


def in_projection(x, norm_g, w_in):
    T, D = x.shape
    assert NA_WIDTH == RET_WIDTH and T % PROJ_TOKENS == 0
    n = N_NA_STREAMS + N_RET_STREAMS
    assert w_in.shape == (D, n * NA_WIDTH)
    tok = pl.BlockSpec((PROJ_TOKENS, D), lambda t: (t, 0))
    out = pl.BlockSpec((PROJ_TOKENS, NA_WIDTH), lambda t: (t, 0))
    return pl.pallas_call(
        _in_proj_kernel,
        out_shape=tuple(jax.ShapeDtypeStruct((T, NA_WIDTH), jnp.float32) for _ in range(n)),
        grid=(T // PROJ_TOKENS,),
        in_specs=[tok, pl.BlockSpec((1, D), lambda t: (0, 0)), pl.BlockSpec(w_in.shape, lambda t: (0, 0))],
        out_specs=tuple(out for _ in range(n)),
        compiler_params=pltpu.CompilerParams(dimension_semantics=("parallel",), vmem_limit_bytes=VMEM_LIMIT),
        name="in_projection",
    )(x, norm_g.reshape(1, D), w_in.astype(MM_DTYPE))


def _out_proj_kernel(a_ref, r_ref, w_ref, x_ref, o_ref):
    o = jnp.dot(a_ref[...].astype(w_ref.dtype), w_ref[:NA_WIDTH, :], preferred_element_type=jnp.float32)
    o = o + jnp.dot(r_ref[...].astype(w_ref.dtype), w_ref[NA_WIDTH:, :], preferred_element_type=jnp.float32)
    o_ref[...] = x_ref[...] + o


def out_projection(na_o, ret_o, w_out, x):
    T, D = x.shape
    tok = pl.BlockSpec((PROJ_TOKENS, D), lambda t: (t, 0))
    half = pl.BlockSpec((PROJ_TOKENS, NA_WIDTH), lambda t: (t, 0))
    return pl.pallas_call(
        _out_proj_kernel,
        out_shape=jax.ShapeDtypeStruct((T, D), jnp.float32),
        grid=(T // PROJ_TOKENS,),
        in_specs=[half, half, pl.BlockSpec(w_out.shape, lambda t: (0, 0)), tok],
        out_specs=tok,
        compiler_params=pltpu.CompilerParams(dimension_semantics=("parallel",), vmem_limit_bytes=VMEM_LIMIT),
        name="out_projection",
    )(na_o, ret_o, w_out.astype(MM_DTYPE), x)


def encoder_layer(x, norm1_g, w_in, na_rpb, ret_decay_fwd, ret_decay_bwd, ret_norm_g,
                  w_out, norm2_g, peer_w_q, peer_sub_keys, peer_u, peer_v, final_g):
    B, L, D = x.shape
    xt = x.reshape(B * L, D)
    streams = [s.reshape(B, L, NA_WIDTH) for s in in_projection(xt, norm1_g, w_in)]
    na_o = neighbourhood_attention_pallas(*streams[:N_NA_STREAMS], na_rpb)
    ret_o = retention_pallas(*streams[N_NA_STREAMS:], ret_decay_fwd, ret_decay_bwd, ret_norm_g)
    x1 = out_projection(na_o.reshape(B * L, NA_WIDTH), ret_o.reshape(B * L, RET_WIDTH), w_out, xt)
    y = peer_and_final_norm(x1, norm2_g, peer_w_q, peer_sub_keys, peer_u, peer_v, final_g)
    return y.reshape(B, L, D)


def kernel(x_prompt, x_sample, norm1_g, w_in, na_rpb, ret_decay_fwd, ret_decay_bwd, ret_norm_g,
           w_out, norm2_g, peer_w_q, peer_sub_keys, peer_u, peer_v, final_g):
    assert DEPTH == 1

    def trunk(x):
        return encoder_layer(x, norm1_g[0], w_in[0], na_rpb[0], ret_decay_fwd[0], ret_decay_bwd[0],
                             ret_norm_g[0], w_out[0], norm2_g[0], peer_w_q[0], peer_sub_keys[0],
                             peer_u[0], peer_v[0], final_g)

    return (trunk(x_prompt), trunk(x_sample))
```

```python
import functools
import math
import jax, jax.numpy as jnp
from jax import lax
import numpy as np
from jax.experimental import pallas as pl
from jax.experimental.pallas import tpu as pltpu

D_MODEL = 1024
DEPTH = 1
GRID_W = 64
NA_HEADS = 8
NA_HEAD_DIM = 64
NA_WIDTH = NA_HEADS * NA_HEAD_DIM
NA_ROWS = 8
NA_COLS = 16
RET_HEADS = 4
RET_HEAD_DIM = 128
RET_WIDTH = RET_HEADS * RET_HEAD_DIM
RET_CHUNK = 128
ROPE_BASE = 10000.0
PEER_HEADS = 8
PEER_QDIM = 256
PEER_HALF = PEER_QDIM // 2
N_KEYS = 128
N_EXPERTS = N_KEYS * N_KEYS
PEER_TOPK = 16
PEER_BLOCK = 128
EPS = 1e-6


def rmsnorm(x, g):
    xf = x.astype(jnp.float32)
    y = xf * lax.rsqrt(jnp.mean(xf * xf, axis=-1, keepdims=True) + EPS)
    return (y * g.astype(jnp.float32)).astype(x.dtype)


def rotary(x):
    L, d = x.shape[1], x.shape[-1]
    inv = 1.0 / (ROPE_BASE ** (jnp.arange(0, d, 2, dtype=jnp.float32) / d))
    ang = jnp.arange(L, dtype=jnp.float32)[:, None] * inv[None, :]
    cos = jnp.cos(ang)[None, :, None, :]
    sin = jnp.sin(ang)[None, :, None, :]
    x1 = x[..., : d // 2].astype(jnp.float32)
    x2 = x[..., d // 2:].astype(jnp.float32)
    return jnp.concatenate([x1 * cos - x2 * sin, x1 * sin + x2 * cos], axis=-1).astype(x.dtype)


def neighbourhood_attention(q, k, v, rpb):
    B, L, H, dh = q.shape
    rows = L // GRID_W
    kr = min(NA_ROWS, rows)
    qg = q.reshape(B, rows, GRID_W, H, dh)
    kg = k.reshape(B, rows, GRID_W, H, dh)
    vg = v.reshape(B, rows, GRID_W, H, dh)
    r_idx = jnp.arange(rows)
    row_start = jnp.clip(r_idx - kr // 2, 0, rows - kr)
    c_idx = jnp.arange(GRID_W)
    col_start = jnp.clip(c_idx - NA_COLS // 2, 0, GRID_W - NA_COLS)
    col_idx = col_start[:, None] + jnp.arange(NA_COLS)[None, :]
    col_bias_idx = col_idx - c_idx[:, None] + (NA_COLS - 1)
    scale = dh ** -0.5

    def one_row(args):
        qr, rs, ri = args
        k_band = lax.dynamic_slice_in_dim(kg, rs, kr, axis=1)
        v_band = lax.dynamic_slice_in_dim(vg, rs, kr, axis=1)
        k_nb = k_band[:, :, col_idx]
        v_nb = v_band[:, :, col_idx]
        s = jnp.einsum('bqhd,brqjhd->bhqrj', qr, k_nb).astype(jnp.float32) * scale
        row_bias_idx = rs + jnp.arange(kr) - ri + (NA_ROWS - 1)
        bias = rpb[:, row_bias_idx][:, :, col_bias_idx]
        s = s + jnp.transpose(bias, (0, 2, 1, 3)).astype(jnp.float32)[None]
        p = jax.nn.softmax(s.reshape(B, H, GRID_W, kr * NA_COLS), axis=-1)
        p = p.reshape(B, H, GRID_W, kr, NA_COLS).astype(v.dtype)
        return jnp.einsum('bhqrj,brqjhd->bqhd', p, v_nb)

    out = lax.map(one_row, (jnp.transpose(qg, (1, 0, 2, 3, 4)), row_start, r_idx))
    return jnp.transpose(out, (1, 0, 2, 3, 4)).reshape(B, L, H * dh)


def chunk_retention(q, k, v, log_gamma, strict):
    B, L, H, dk = q.shape
    dv = v.shape[-1]
    C = RET_CHUNK
    nc = L // C
    dt = q.dtype
    qc = q.reshape(B, nc, C, H, dk)
    kc = k.reshape(B, nc, C, H, dk)
    vc = v.reshape(B, nc, C, H, dv)
    pos = jnp.arange(C, dtype=jnp.float32)
    diff = pos[:, None] - pos[None, :]
    mask = (diff > 0) if strict else (diff >= 0)
    decay = jnp.where(mask[None], jnp.exp(log_gamma[:, None, None] * jnp.maximum(diff, 0.0)[None]), 0.0).astype(dt)
    k_decay = jnp.exp(log_gamma[None, :] * (C - 1 - pos)[:, None]).astype(dt)
    q_decay = jnp.exp(log_gamma[None, :] * (pos + 1)[:, None]).astype(dt)
    chunk_decay = jnp.exp(log_gamma * C).astype(dt)
    att = jnp.einsum('bnqhd,bnkhd->bnhqk', qc, kc) * decay[None, None]
    o_intra = jnp.einsum('bnhqk,bnkhe->bnqhe', att, vc)
    S = jnp.einsum('bnkhd,kh,bnkhe->nbhde', kc, k_decay, vc)

    def step(R, S_i):
        return chunk_decay[None, :, None, None] * R + S_i, R

    _, R_prev = lax.scan(step, jnp.zeros(S.shape[1:], S.dtype), S)
    o_inter = jnp.einsum('bnqhd,qh,nbhde->bnqhe', qc, q_decay, R_prev)
    return (o_intra + o_inter).reshape(B, L, H, dv)


def bidirectional_retention(q, k, v, gate, decay_fwd, decay_bwd, norm_g):
    B, L, H, dk = q.shape
    q = rotary(q)
    k = rotary(k) * (dk ** -0.5)
    lg_f = jax.nn.log_sigmoid(decay_fwd.astype(jnp.float32))
    lg_b = jax.nn.log_sigmoid(decay_bwd.astype(jnp.float32))
    o_f = chunk_retention(q, k, v, lg_f, False)
    o_b = jnp.flip(chunk_retention(jnp.flip(q, 1), jnp.flip(k, 1), jnp.flip(v, 1), lg_b, True), 1)
    o = (o_f + o_b).astype(jnp.float32)
    mu = jnp.mean(o, axis=-1, keepdims=True)
    var = jnp.mean(jnp.square(o - mu), axis=-1, keepdims=True)
    o = (o - mu) * lax.rsqrt(var + EPS)
    o = o.reshape(B, L, H * v.shape[-1]) * norm_g.astype(jnp.float32)
    return (jax.nn.silu(gate.astype(jnp.float32)) * o).astype(v.dtype)


MM_DTYPE = jnp.bfloat16
GATE_DTYPE = jnp.bfloat16
GATE_PACK = 2
LANES = 128
NOT_TOP = 255.0
SEL_TOKENS = 256
MIX_TOKENS = 512
MIX_KEYS = 8
VMEM_LIMIT = 56 * 1024 * 1024


def _dot_nt(a, b):
    return lax.dot_general(a, b, (((1,), (1,)), ((), ())), preferred_element_type=jnp.float32)


def _topk_rank(s):
    iota = lax.broadcasted_iota(jnp.int32, s.shape, 0).astype(jnp.float32)
    rank = jnp.full(s.shape, NOT_TOP, jnp.float32)
    tops = []
    for r in range(PEER_TOPK):
        m = jnp.max(s, axis=0, keepdims=True)
        idx = jnp.min(jnp.where(s == m, iota, float(s.shape[0])), axis=0, keepdims=True)
        hit = iota == idx
        rank = jnp.where(hit, float(r), rank)
        s = jnp.where(hit, -jnp.inf, s)
        tops.append(m)
    return rank, tops


def _young_counts(a_top, b_top):
    k = PEER_TOPK
    w = a_top[0].shape[1]
    b16 = jnp.concatenate(b_top, axis=0)
    b8 = jnp.concatenate(b_top[:8], axis=0)
    blocks = [a_top[0] + b16] + [a_top[a] + b8 for a in range(1, 8)]
    blocks.append(jnp.concatenate(a_top[8:], axis=0) + b_top[0])
    cand = jnp.concatenate(blocks, axis=0)
    rows = cand.shape[0]
    rho = lax.broadcasted_iota(jnp.int32, (rows, w), 0)
    mid = rho - 16
    a_id = jnp.where(rho < 16, 0, jnp.where(rho < 72, 1 + (mid >> 3), rho - 64))
    b_id = jnp.where(rho < 16, rho, jnp.where(rho < 72, mid & 7, 0))
    pos = (a_id * k + b_id).astype(jnp.float32)
    cand = jnp.where((a_id + 1) * (b_id + 1) <= k, cand, -jnp.inf)
    a_iota = lax.broadcasted_iota(jnp.int32, (k, w), 0).astype(jnp.float32)
    n = jnp.zeros((k, w), jnp.float32)
    best = []
    for _ in range(k):
        m = jnp.max(cand, axis=0, keepdims=True)
        p = jnp.min(jnp.where(cand == m, pos, float(k * k)), axis=0, keepdims=True)
        cand = jnp.where(pos == p, -jnp.inf, cand)
        n = n + (a_iota == jnp.floor(p * (1.0 / k))).astype(jnp.float32)
        best.append(m)
    return n, best


def _peer_select_kernel(x_ref, g_ref, wq_ref, keys_ref, h_ref, r1_ref, e1_ref, n0_ref, c0_ref, hb_ref):
    hd = pl.program_id(1)

    @pl.when(hd == 0)
    def _():
        x = x_ref[...]
        y = x * lax.rsqrt(jnp.mean(x * x, axis=-1, keepdims=True) + EPS) * g_ref[...]
        hb_ref[...] = y.astype(hb_ref.dtype)
        h_ref[...] = y.astype(h_ref.dtype)

    q = jnp.dot(hb_ref[...], wq_ref[...], preferred_element_type=jnp.float32)
    q = q.astype(keys_ref.dtype)
    for c in range(SEL_TOKENS // LANES):
        qc = q[c * LANES:(c + 1) * LANES, :]
        s0 = _dot_nt(keys_ref[0, 0], qc[:, :PEER_HALF])
        s1 = _dot_nt(keys_ref[0, 1], qc[:, PEER_HALF:])
        rank0, a_top = _topk_rank(s0)
        rank1, b_top = _topk_rank(s1)
        n, best = _young_counts(a_top, b_top)
        z = jnp.zeros_like(best[0])
        for m in best:
            z = z + jnp.exp(m - best[0])
        n0 = jnp.zeros_like(s0)
        for a in range(PEER_TOPK):
            n0 = jnp.where(rank0 == float(a), n[a:a + 1, :], n0)
        sl = slice(c * LANES, (c + 1) * LANES)
        e1 = jnp.where(rank1 < NOT_TOP, jnp.exp(s1 - b_top[0]), 0.0)
        r1_ref[:, sl] = pltpu.bitcast(rank1.astype(GATE_DTYPE), r1_ref.dtype)
        e1_ref[:, sl] = pltpu.bitcast(e1.astype(GATE_DTYPE), e1_ref.dtype)
        n0_ref[:, sl] = n0
        c0_ref[:, sl] = jnp.exp(s0 - a_top[0]) * (0.5 / z)


def _twice_gelu_tanh(x):
    c = math.sqrt(2.0 / math.pi)
    return x * (1.0 + jnp.tanh(x * (c + (c * 0.044715) * (x * x))))


def _peer_mix_kernel(h_ref, u_ref, vt_ref, r1_ref, e1_ref, n0_ref, c0_ref, x_ref, fg_ref, y_ref,
                     acc_ref, a_ref, w_ref):
    e = pl.program_id(1)

    @pl.when(e == 0)
    def _():
        acc_ref[...] = jnp.zeros_like(acc_ref)

    gd = GATE_DTYPE
    pack = SUBLANES * GATE_PACK
    krows = N_KEYS // GATE_PACK
    half = MIX_TOKENS // 2
    for hf in range(2):
        a_ref[:, hf * half:(hf + 1) * half] = _dot_nt(u_ref[...], h_ref[hf * half:(hf + 1) * half, :])
    rows = [pl.ds(pl.multiple_of(hd * N_KEYS + e * MIX_KEYS, MIX_KEYS), MIX_KEYS) for hd in range(PEER_HEADS)]
    for hf in range(2):
        for c in range(hf * half // LANES, (hf + 1) * half // LANES):
            sl = slice(c * LANES, (c + 1) * LANES)
            n_t = [n0_ref[r, sl] for r in rows]
            c_t = [c0_ref[r, sl] for r in rows]
            for ii in range(MIX_KEYS):
                g = jnp.zeros((N_KEYS // pack, pack, LANES), gd)
                for hd in range(PEER_HEADS):
                    n_b = jnp.broadcast_to(n_t[hd][ii:ii + 1, :], (pack, LANES)).astype(gd)[None]
                    c_b = jnp.broadcast_to(c_t[hd][ii:ii + 1, :], (pack, LANES)).astype(gd)[None]
                    r1 = pltpu.bitcast(r1_ref[hd * krows:(hd + 1) * krows, sl], gd).reshape(g.shape)
                    e1 = pltpu.bitcast(e1_ref[hd * krows:(hd + 1) * krows, sl], gd).reshape(g.shape)
                    g = g + jnp.where(r1 < n_b, e1, jnp.zeros_like(e1)) * c_b
                act = _twice_gelu_tanh(a_ref[ii * N_KEYS:(ii + 1) * N_KEYS, sl])
                gate = g.reshape(N_KEYS, LANES).astype(jnp.float32)
                w_ref[ii * N_KEYS:(ii + 1) * N_KEYS, sl] = (act * gate).astype(w_ref.dtype)
        hs = slice(hf * half, (hf + 1) * half)
        acc_ref[:, hs] += jnp.dot(vt_ref[...], w_ref[:, hs], preferred_element_type=jnp.float32)

    @pl.when(e == pl.num_programs(1) - 1)
    def _():
        o = acc_ref[...].T + x_ref[...]
        y_ref[...] = o * lax.rsqrt(jnp.mean(o * o, axis=-1, keepdims=True) + EPS) * fg_ref[...]


def peer_and_final_norm(x, norm2_g, w_q, sub_keys, u, v, final_g):
    T, D = x.shape
    assert T % MIX_TOKENS == 0 and T % SEL_TOKENS == 0 and N_KEYS % MIX_KEYS == 0
    f32 = jnp.float32
    nq = PEER_HEADS * PEER_QDIM
    tab = jax.ShapeDtypeStruct((PEER_HEADS * N_KEYS, T), f32)
    gtab = jax.ShapeDtypeStruct((PEER_HEADS * N_KEYS // GATE_PACK, T), jnp.uint32)
    tab_spec = pl.BlockSpec((N_KEYS, SEL_TOKENS), lambda t, h: (h, t))
    gtab_spec = pl.BlockSpec((N_KEYS // GATE_PACK, SEL_TOKENS), lambda t, h: (h, t))
    h, r1, e1, n0, c0 = pl.pallas_call(
        _peer_select_kernel,
        out_shape=(jax.ShapeDtypeStruct((T, D), MM_DTYPE), gtab, gtab, tab, tab),
        grid=(T // SEL_TOKENS, PEER_HEADS),
        in_specs=[pl.BlockSpec((SEL_TOKENS, D), lambda t, h: (t, 0)),
                  pl.BlockSpec((1, D), lambda t, h: (0, 0)),
                  pl.BlockSpec((D, PEER_QDIM), lambda t, h: (0, h)),
                  pl.BlockSpec((1, 2, N_KEYS, PEER_HALF), lambda t, h: (h, 0, 0, 0))],
        out_specs=(pl.BlockSpec((SEL_TOKENS, D), lambda t, h: (t, 0)), gtab_spec, gtab_spec, tab_spec, tab_spec),
        scratch_shapes=[pltpu.VMEM((SEL_TOKENS, D), MM_DTYPE)],
        compiler_params=pltpu.CompilerParams(dimension_semantics=("parallel", "arbitrary"),
                                             vmem_limit_bytes=VMEM_LIMIT),
        name="peer_select",
    )(x, norm2_g.reshape(1, D), w_q.astype(MM_DTYPE), sub_keys.astype(MM_DTYPE))

    eb = MIX_KEYS * N_KEYS
    mix_tab = pl.BlockSpec((PEER_HEADS * N_KEYS, MIX_TOKENS), lambda t, e: (0, t))
    mix_gtab = pl.BlockSpec((PEER_HEADS * N_KEYS // GATE_PACK, MIX_TOKENS), lambda t, e: (0, t))
    tok_blk = pl.BlockSpec((MIX_TOKENS, D), lambda t, e: (t, 0))
    return pl.pallas_call(
        _peer_mix_kernel,
        out_shape=jax.ShapeDtypeStruct((T, D), f32),
        grid=(T // MIX_TOKENS, N_KEYS // MIX_KEYS),
        in_specs=[tok_blk,
                  pl.BlockSpec((eb, D), lambda t, e: (e, 0)),
                  pl.BlockSpec((D, eb), lambda t, e: (0, e)),
                  mix_gtab, mix_gtab, mix_tab, mix_tab,
                  tok_blk,
                  pl.BlockSpec((1, D), lambda t, e: (0, 0))],
        out_specs=tok_blk,
        scratch_shapes=[pltpu.VMEM((D, MIX_TOKENS), f32),
                        pltpu.VMEM((eb, MIX_TOKENS), f32),
                        pltpu.VMEM((eb, MIX_TOKENS), MM_DTYPE)],
        compiler_params=pltpu.CompilerParams(dimension_semantics=("parallel", "arbitrary"),
                                             vmem_limit_bytes=VMEM_LIMIT),
        name="peer_mix",
    )(h, u.astype(MM_DTYPE), v.T.astype(MM_DTYPE), r1, e1, n0, c0, x, final_g.reshape(1, D))


NA_BAND = NA_ROWS * GRID_W
NA_MASKED = -1e30


def _na_bias_table(rpb):
    cq = jnp.arange(GRID_W)[:, None]
    ck = jnp.arange(GRID_W)[None, :]
    cs = jnp.clip(cq - NA_COLS // 2, 0, GRID_W - NA_COLS)
    valid = (ck >= cs) & (ck < cs + NA_COLS)
    cb = jnp.clip(ck - cq + NA_COLS - 1, 0, 2 * NA_COLS - 2)
    rb = jnp.arange(NA_ROWS)[:, None] + jnp.arange(NA_ROWS)[None, :]
    tab = rpb[:, rb][:, :, :, cb]
    tab = jnp.where(valid, tab, NA_MASKED)
    return jnp.transpose(tab, (1, 0, 3, 2, 4)).reshape(NA_ROWS, NA_HEADS, GRID_W, NA_BAND)


def _na_row_start(r, rows):
    return jnp.clip(r - NA_ROWS // 2, 0, rows - NA_ROWS)


def _na_kernel(q_ref, k_ref, v_ref, tab_ref, o_ref, *, rows):
    start = pl.multiple_of(_na_row_start(pl.program_id(1), rows) * GRID_W, GRID_W)
    kb = k_ref[0, pl.ds(start, NA_BAND), :]
    vb = v_ref[0, pl.ds(start, NA_BAND), :]
    q = q_ref[0]
    lane = lax.broadcasted_iota(jnp.int32, (GRID_W, LANES), 1)
    lo = lane < NA_HEAD_DIM
    pairs = [slice(hp * LANES, (hp + 1) * LANES) for hp in range(NA_HEADS // 2)]
    scores = []
    for h in range(NA_HEADS):
        qp = q[:, pairs[h // 2]]
        mine = lo if h % 2 == 0 else jnp.logical_not(lo)
        scores.append(_dot_nt(jnp.where(mine, qp, jnp.zeros_like(qp)), kb[:, pairs[h // 2]]) + tab_ref[0, h])
    probs, sums = [], []
    for sc in scores:
        p = jnp.exp(sc - jnp.max(sc, axis=-1, keepdims=True))
        sums.append(jnp.sum(p, axis=-1, keepdims=True))
        probs.append(p.astype(vb.dtype))
    outs = []
    for hp in range(NA_HEADS // 2):
        o = [jnp.dot(probs[2 * hp + s], vb[:, pairs[hp]], preferred_element_type=jnp.float32) / sums[2 * hp + s]
             for s in range(2)]
        outs.append(jnp.where(lo, o[0], o[1]))
    o_ref[0] = jnp.concatenate(outs, axis=-1).astype(o_ref.dtype)


def neighbourhood_attention_pallas(q, k, v, rpb):
    B, L, W = q.shape
    rows = L // GRID_W
    assert rows >= NA_ROWS and W == NA_HEADS * NA_HEAD_DIM and 2 * NA_HEAD_DIM == LANES
    q = (q * NA_HEAD_DIM ** -0.5).astype(MM_DTYPE)
    seq = pl.BlockSpec((1, L, W), lambda b, r: (b, 0, 0))
    row = pl.BlockSpec((1, GRID_W, W), lambda b, r: (b, r, 0))
    tab = pl.BlockSpec((1, NA_HEADS, GRID_W, NA_BAND),
                       lambda b, r: (_na_row_start(r, rows) - r + NA_ROWS - 1, 0, 0, 0))
    return pl.pallas_call(
        functools.partial(_na_kernel, rows=rows),
        out_shape=jax.ShapeDtypeStruct((B, L, W), jnp.float32),
        grid=(B, rows),
        in_specs=[row, seq, seq, tab],
        out_specs=row,
        compiler_params=pltpu.CompilerParams(dimension_semantics=("parallel", "arbitrary"),
                                             vmem_limit_bytes=VMEM_LIMIT),
        name="na_attention",
    )(q, k.astype(MM_DTYPE), v.astype(MM_DTYPE), _na_bias_table(rpb))


SUBLANES = 8


def _rotate_half(x, cos, sin_signed):
    return x * cos + pltpu.roll(x, x.shape[-1] // 2, axis=1) * sin_signed


def _dot_tn(a, b):
    return lax.dot_general(a, b, (((0,), (0,)), ((), ())), preferred_element_type=jnp.float32)


def _ret_kernel(q_ref, k_ref, v_ref, g_ref, cos_ref, sin_ref, df_ref, db_ref, ng_ref, o_ref,
                of_ref, ob_ref, *, nc):
    C = RET_CHUNK
    mm = o_ref.dtype
    lg_f = jax.nn.log_sigmoid(df_ref[0])[0:1, :]
    lg_b = jax.nn.log_sigmoid(db_ref[0])[0:1, :]
    pos = lax.broadcasted_iota(jnp.int32, (C, C), 0).astype(jnp.float32)
    diff = pos - lax.broadcasted_iota(jnp.int32, (C, C), 1).astype(jnp.float32)
    dec_f = jnp.where(diff >= 0, jnp.exp(lg_f * jnp.maximum(diff, 0.0)), 0.0)
    dec_b = jnp.where(diff < 0, jnp.exp(lg_b * jnp.maximum(-diff, 0.0)), 0.0)
    kdec_f = jnp.exp(lg_f * (C - 1 - pos))
    qdec_f = jnp.exp(lg_f * (pos + 1))
    kdec_b = jnp.exp(lg_b * pos)
    qdec_b = jnp.exp(lg_b * (C - pos))
    step_f = jnp.exp(lg_f * C)
    step_b = jnp.exp(lg_b * C)
    scale = RET_HEAD_DIM ** -0.5

    def load(c):
        sl = pl.ds(pl.multiple_of(c * C, C), C)
        q = _rotate_half(q_ref[0, sl, :], cos_ref[sl, :], sin_ref[sl, :])
        k = _rotate_half(k_ref[0, sl, :], cos_ref[sl, :], sin_ref[sl, :]) * scale
        return sl, q, k, v_ref[0, sl, :].astype(mm)

    def body(c, carry):
        rf, rb = carry
        sl_f, q_f, k_f, v_f = load(c)
        sl_b, q_b, k_b, v_b = load(nc - 1 - c)
        att_f = _dot_nt(q_f.astype(mm), k_f.astype(mm))
        att_b = _dot_nt(q_b.astype(mm), k_b.astype(mm))
        inter_f = jnp.dot((q_f * qdec_f).astype(mm), rf.astype(mm), preferred_element_type=jnp.float32)
        inter_b = jnp.dot((q_b * qdec_b).astype(mm), rb.astype(mm), preferred_element_type=jnp.float32)
        s_f = _dot_tn((k_f * kdec_f).astype(mm), v_f)
        s_b = _dot_tn((k_b * kdec_b).astype(mm), v_b)
        of_ref[sl_f, :] = inter_f + jnp.dot((att_f * dec_f).astype(mm), v_f, preferred_element_type=jnp.float32)
        ob_ref[sl_b, :] = inter_b + jnp.dot((att_b * dec_b).astype(mm), v_b, preferred_element_type=jnp.float32)
        return step_f * rf + s_f, step_b * rb + s_b

    zero = jnp.zeros((RET_HEAD_DIM, RET_HEAD_DIM), jnp.float32)
    lax.fori_loop(0, nc, body, (zero, zero))

    def finish(c, carry):
        sl = pl.ds(pl.multiple_of(c * C, C), C)
        o = of_ref[sl, :] + ob_ref[sl, :]
        mu = jnp.mean(o, axis=-1, keepdims=True)
        var = jnp.mean(jnp.square(o - mu), axis=-1, keepdims=True)
        o = (o - mu) * lax.rsqrt(var + EPS) * ng_ref[...]
        o_ref[0, sl, :] = (jax.nn.silu(g_ref[0, sl, :]) * o).astype(mm)
        return carry

    lax.fori_loop(0, nc, finish, 0)


def retention_pallas(q, k, v, gate, decay_fwd, decay_bwd, norm_g):
    B, L, W = q.shape
    dh = RET_HEAD_DIM
    assert W == RET_HEADS * dh and dh == LANES and L % RET_CHUNK == 0
    inv = 1.0 / (ROPE_BASE ** (jnp.arange(0, dh, 2, dtype=jnp.float32) / dh))
    ang = jnp.arange(L, dtype=jnp.float32)[:, None] * inv[None, :]
    cos = jnp.concatenate([jnp.cos(ang), jnp.cos(ang)], axis=-1)
    sin = jnp.concatenate([-jnp.sin(ang), jnp.sin(ang)], axis=-1)
    bcast = lambda d: jnp.broadcast_to(d.astype(jnp.float32)[:, None, None], (RET_HEADS, SUBLANES, dh))
    seq = pl.BlockSpec((1, L, dh), lambda b, h: (b, 0, h))
    tab = pl.BlockSpec((L, dh), lambda b, h: (0, 0))
    dec = pl.BlockSpec((1, SUBLANES, dh), lambda b, h: (h, 0, 0))
    return pl.pallas_call(
        functools.partial(_ret_kernel, nc=L // RET_CHUNK),
        out_shape=jax.ShapeDtypeStruct((B, L, W), MM_DTYPE),
        grid=(B, RET_HEADS),
        in_specs=[seq, seq, seq, seq, tab, tab, dec, dec, pl.BlockSpec((1, dh), lambda b, h: (0, h))],
        out_specs=seq,
        scratch_shapes=[pltpu.VMEM((L, dh), jnp.float32), pltpu.VMEM((L, dh), jnp.float32)],
        compiler_params=pltpu.CompilerParams(dimension_semantics=("parallel", "parallel"),
                                             vmem_limit_bytes=VMEM_LIMIT),
        name="retention",
    )(q, k, v, gate, cos, sin, bcast(decay_fwd), bcast(decay_bwd), norm_g.reshape(1, W))


PROJ_TOKENS = 512
N_NA_STREAMS = 3
N_RET_STREAMS = 4


def _in_proj_kernel(x_ref, g_ref, w_ref, *out_refs):
    x = x_ref[...]
    h = (x * lax.rsqrt(jnp.mean(x * x, axis=-1, keepdims=True) + EPS) * g_ref[...]).astype(w_ref.dtype)
    for j, o_ref in enumerate(out_refs):
        o_ref[...] = jnp.dot(h, w_ref[:, j * NA_WIDTH:(j + 1) * NA_WIDTH],
                             preferred_element_type=jnp.float32).astype(o_ref.dtype)


def in_projection(x, norm_g, w_in):
    T, D = x.shape
    assert NA_WIDTH == RET_WIDTH and T % PROJ_TOKENS == 0
    n = N_NA_STREAMS + N_RET_STREAMS
    assert w_in.shape == (D, n * NA_WIDTH)
    tok = pl.BlockSpec((PROJ_TOKENS, D), lambda t: (t, 0))
    out = pl.BlockSpec((PROJ_TOKENS, NA_WIDTH), lambda t: (t, 0))
    return pl.pallas_call(
        _in_proj_kernel,
        out_shape=tuple(jax.ShapeDtypeStruct((T, NA_WIDTH), jnp.float32) for _ in range(n)),
        grid=(T // PROJ_TOKENS,),
        in_specs=[tok, pl.BlockSpec((1, D), lambda t: (0, 0)), pl.BlockSpec(w_in.shape, lambda t: (0, 0))],
        out_specs=tuple(out for _ in range(n)),
        compiler_params=pltpu.CompilerParams(dimension_semantics=("parallel",), vmem_limit_bytes=VMEM_LIMIT),
        name="in_projection",
    )(x, norm_g.reshape(1, D), w_in.astype(MM_DTYPE))


def _out_proj_kernel(a_ref, r_ref, w_ref, x_ref, o_ref):
    o = jnp.dot(a_ref[...].astype(w_ref.dtype), w_ref[:NA_WIDTH, :], preferred_element_type=jnp.float32)
    o = o + jnp.dot(r_ref[...].astype(w_ref.dtype), w_ref[NA_WIDTH:, :], preferred_element_type=jnp.float32)
    o_ref[...] = x_ref[...] + o


def out_projection(na_o, ret_o, w_out, x):
    T, D = x.shape
    tok = pl.BlockSpec((PROJ_TOKENS, D), lambda t: (t, 0))
    half = pl.BlockSpec((PROJ_TOKENS, NA_WIDTH), lambda t: (t, 0))
    return pl.pallas_call(
        _out_proj_kernel,
        out_shape=jax.ShapeDtypeStruct((T, D), jnp.float32),
        grid=(T // PROJ_TOKENS,),
        in_specs=[half, half, pl.BlockSpec(w_out.shape, lambda t: (0, 0)), tok],
        out_specs=tok,
        compiler_params=pltpu.CompilerParams(dimension_semantics=("parallel",), vmem_limit_bytes=VMEM_LIMIT),
        name="out_projection",
    )(na_o, ret_o, w_out.astype(MM_DTYPE), x)


def encoder_layer(x, norm1_g, w_in, na_rpb, ret_decay_fwd, ret_decay_bwd, ret_norm_g,
                  w_out, norm2_g, peer_w_q, peer_sub_keys, peer_u, peer_v, final_g):
    B, L, D = x.shape
    xt = x.reshape(B * L, D)
    streams = [s.reshape(B, L, NA_WIDTH) for s in in_projection(xt, norm1_g, w_in)]
    na_o = neighbourhood_attention_pallas(*streams[:N_NA_STREAMS], na_rpb)
    ret_o = retention_pallas(*streams[N_NA_STREAMS:], ret_decay_fwd, ret_decay_bwd, ret_norm_g)
    x1 = out_projection(na_o.reshape(B * L, NA_WIDTH), ret_o.reshape(B * L, RET_WIDTH), w_out, xt)
    y = peer_and_final_norm(x1, norm2_g, peer_w_q, peer_sub_keys, peer_u, peer_v, final_g)
    return y.reshape(B, L, D)


def kernel(x_prompt, x_sample, norm1_g, w_in, na_rpb, ret_decay_fwd, ret_decay_bwd, ret_norm_g,
           w_out, norm2_g, peer_w_q, peer_sub_keys, peer_u, peer_v, final_g):
    assert DEPTH == 1

    def trunk(x):
        return encoder_layer(x, norm1_g[0], w_in[0], na_rpb[0], ret_decay_fwd[0], ret_decay_bwd[0],
                             ret_norm_g[0], w_out[0], norm2_g[0], peer_w_q[0], peer_sub_keys[0],
                             peer_u[0], peer_v[0], final_g)

    return (trunk(x_prompt), trunk(x_sample))
```

```python
import functools
import math
import jax, jax.numpy as jnp
from jax import lax
import numpy as np
from jax.experimental import pallas as pl
from jax.experimental.pallas import tpu as pltpu

D_MODEL = 1024
DEPTH = 1
GRID_W = 64
NA_HEADS = 8
NA_HEAD_DIM = 64
NA_WIDTH = NA_HEADS * NA_HEAD_DIM
NA_ROWS = 8
NA_COLS = 16
RET_HEADS = 4
RET_HEAD_DIM = 128
RET_WIDTH = RET_HEADS * RET_HEAD_DIM
RET_CHUNK = 128
ROPE_BASE = 10000.0
PEER_HEADS = 8
PEER_QDIM = 256
PEER_HALF = PEER_QDIM // 2
N_KEYS = 128
N_EXPERTS = N_KEYS * N_KEYS
PEER_TOPK = 16
PEER_BLOCK = 128
EPS = 1e-6


def rmsnorm(x, g):
    xf = x.astype(jnp.float32)
    y = xf * lax.rsqrt(jnp.mean(xf * xf, axis=-1, keepdims=True) + EPS)
    return (y * g.astype(jnp.float32)).astype(x.dtype)


def rotary(x):
    L, d = x.shape[1], x.shape[-1]
    inv = 1.0 / (ROPE_BASE ** (jnp.arange(0, d, 2, dtype=jnp.float32) / d))
    ang = jnp.arange(L, dtype=jnp.float32)[:, None] * inv[None, :]
    cos = jnp.cos(ang)[None, :, None, :]
    sin = jnp.sin(ang)[None, :, None, :]
    x1 = x[..., : d // 2].astype(jnp.float32)
    x2 = x[..., d // 2:].astype(jnp.float32)
    return jnp.concatenate([x1 * cos - x2 * sin, x1 * sin + x2 * cos], axis=-1).astype(x.dtype)


def neighbourhood_attention(q, k, v, rpb):
    B, L, H, dh = q.shape
    rows = L // GRID_W
    kr = min(NA_ROWS, rows)
    qg = q.reshape(B, rows, GRID_W, H, dh)
    kg = k.reshape(B, rows, GRID_W, H, dh)
    vg = v.reshape(B, rows, GRID_W, H, dh)
    r_idx = jnp.arange(rows)
    row_start = jnp.clip(r_idx - kr // 2, 0, rows - kr)
    c_idx = jnp.arange(GRID_W)
    col_start = jnp.clip(c_idx - NA_COLS // 2, 0, GRID_W - NA_COLS)
    col_idx = col_start[:, None] + jnp.arange(NA_COLS)[None, :]
    col_bias_idx = col_idx - c_idx[:, None] + (NA_COLS - 1)
    scale = dh ** -0.5

    def one_row(args):
        qr, rs, ri = args
        k_band = lax.dynamic_slice_in_dim(kg, rs, kr, axis=1)
        v_band = lax.dynamic_slice_in_dim(vg, rs, kr, axis=1)
        k_nb = k_band[:, :, col_idx]
        v_nb = v_band[:, :, col_idx]
        s = jnp.einsum('bqhd,brqjhd->bhqrj', qr, k_nb).astype(jnp.float32) * scale
        row_bias_idx = rs + jnp.arange(kr) - ri + (NA_ROWS - 1)
        bias = rpb[:, row_bias_idx][:, :, col_bias_idx]
        s = s + jnp.transpose(bias, (0, 2, 1, 3)).astype(jnp.float32)[None]
        p = jax.nn.softmax(s.reshape(B, H, GRID_W, kr * NA_COLS), axis=-1)
        p = p.reshape(B, H, GRID_W, kr, NA_COLS).astype(v.dtype)
        return jnp.einsum('bhqrj,brqjhd->bqhd', p, v_nb)

    out = lax.map(one_row, (jnp.transpose(qg, (1, 0, 2, 3, 4)), row_start, r_idx))
    return jnp.transpose(out, (1, 0, 2, 3, 4)).reshape(B, L, H * dh)


def chunk_retention(q, k, v, log_gamma, strict):
    B, L, H, dk = q.shape
    dv = v.shape[-1]
    C = RET_CHUNK
    nc = L // C
    dt = q.dtype
    qc = q.reshape(B, nc, C, H, dk)
    kc = k.reshape(B, nc, C, H, dk)
    vc = v.reshape(B, nc, C, H, dv)
    pos = jnp.arange(C, dtype=jnp.float32)
    diff = pos[:, None] - pos[None, :]
    mask = (diff > 0) if strict else (diff >= 0)
    decay = jnp.where(mask[None], jnp.exp(log_gamma[:, None, None] * jnp.maximum(diff, 0.0)[None]), 0.0).astype(dt)
    k_decay = jnp.exp(log_gamma[None, :] * (C - 1 - pos)[:, None]).astype(dt)
    q_decay = jnp.exp(log_gamma[None, :] * (pos + 1)[:, None]).astype(dt)
    chunk_decay = jnp.exp(log_gamma * C).astype(dt)
    att = jnp.einsum('bnqhd,bnkhd->bnhqk', qc, kc) * decay[None, None]
    o_intra = jnp.einsum('bnhqk,bnkhe->bnqhe', att, vc)
    S = jnp.einsum('bnkhd,kh,bnkhe->nbhde', kc, k_decay, vc)

    def step(R, S_i):
        return chunk_decay[None, :, None, None] * R + S_i, R

    _, R_prev = lax.scan(step, jnp.zeros(S.shape[1:], S.dtype), S)
    o_inter = jnp.einsum('bnqhd,qh,nbhde->bnqhe', qc, q_decay, R_prev)
    return (o_intra + o_inter).reshape(B, L, H, dv)


def bidirectional_retention(q, k, v, gate, decay_fwd, decay_bwd, norm_g):
    B, L, H, dk = q.shape
    q = rotary(q)
    k = rotary(k) * (dk ** -0.5)
    lg_f = jax.nn.log_sigmoid(decay_fwd.astype(jnp.float32))
    lg_b = jax.nn.log_sigmoid(decay_bwd.astype(jnp.float32))
    o_f = chunk_retention(q, k, v, lg_f, False)
    o_b = jnp.flip(chunk_retention(jnp.flip(q, 1), jnp.flip(k, 1), jnp.flip(v, 1), lg_b, True), 1)
    o = (o_f + o_b).astype(jnp.float32)
    mu = jnp.mean(o, axis=-1, keepdims=True)
    var = jnp.mean(jnp.square(o - mu), axis=-1, keepdims=True)
    o = (o - mu) * lax.rsqrt(var + EPS)
    o = o.reshape(B, L, H * v.shape[-1]) * norm_g.astype(jnp.float32)
    return (jax.nn.silu(gate.astype(jnp.float32)) * o).astype(v.dtype)


MM_DTYPE = jnp.bfloat16
GATE_DTYPE = jnp.bfloat16
GATE_PACK = 2
LANES = 128
NOT_TOP = 255.0
SEL_TOKENS = 512
MIX_TOKENS = 512
MIX_KEYS = 16
VMEM_LIMIT = 56 * 1024 * 1024


def _dot_nt(a, b):
    return lax.dot_general(a, b, (((1,), (1,)), ((), ())), preferred_element_type=jnp.float32)


def _topk_rank(s):
    iota = lax.broadcasted_iota(jnp.int32, s.shape, 0).astype(jnp.float32)
    rank = jnp.full(s.shape, NOT_TOP, jnp.float32)
    tops = []
    for r in range(PEER_TOPK):
        m = jnp.max(s, axis=0, keepdims=True)
        idx = jnp.min(jnp.where(s == m, iota, float(s.shape[0])), axis=0, keepdims=True)
        hit = iota == idx
        rank = jnp.where(hit, float(r), rank)
        s = jnp.where(hit, -jnp.inf, s)
        tops.append(m)
    return rank, tops


def _topk_rank_untied(s):
    rank = jnp.full(s.shape, NOT_TOP, jnp.float32)
    tops = []
    for r in range(PEER_TOPK):
        m = jnp.max(s, axis=0, keepdims=True)
        hit = s == m
        rank = jnp.where(hit, float(r), rank)
        s = jnp.where(hit, -jnp.inf, s)
        tops.append(m)
    removed = jnp.sum(jnp.where(rank < NOT_TOP, 1.0, 0.0), axis=0, keepdims=True)
    return rank, tops, removed


def _young_counts(a_top, b_top):
    k = PEER_TOPK
    w = a_top[0].shape[1]
    b16 = jnp.concatenate(b_top, axis=0)
    b8 = jnp.concatenate(b_top[:8], axis=0)
    blocks = [a_top[0] + b16] + [a_top[a] + b8 for a in range(1, 8)]
    blocks.append(jnp.concatenate(a_top[8:], axis=0) + b_top[0])
    cand = jnp.concatenate(blocks, axis=0)
    rows = cand.shape[0]
    rho = lax.broadcasted_iota(jnp.int32, (rows, w), 0)
    mid = rho - 16
    a_id = jnp.where(rho < 16, 0, jnp.where(rho < 72, 1 + (mid >> 3), rho - 64))
    b_id = jnp.where(rho < 16, rho, jnp.where(rho < 72, mid & 7, 0))
    pos = (a_id * k + b_id).astype(jnp.float32)
    cand = jnp.where((a_id + 1) * (b_id + 1) <= k, cand, -jnp.inf)
    a_iota = lax.broadcasted_iota(jnp.int32, (k, w), 0).astype(jnp.float32)
    n = jnp.zeros((k, w), jnp.float32)
    best = []
    for _ in range(k):
        m = jnp.max(cand, axis=0, keepdims=True)
        p = jnp.min(jnp.where(cand == m, pos, float(k * k)), axis=0, keepdims=True)
        cand = jnp.where(pos == p, -jnp.inf, cand)
        n = n + (a_iota == jnp.floor(p * (1.0 / k))).astype(jnp.float32)
        best.append(m)
    return n, best


def _peer_select_kernel(x_ref, g_ref, wq_ref, keys_ref, h_ref, r1_ref, e1_ref, n0_ref, c0_ref, hb_ref):
    hd = pl.program_id(1)

    @pl.when(hd == 0)
    def _():
        x = x_ref[...]
        y = x * lax.rsqrt(jnp.mean(x * x, axis=-1, keepdims=True) + EPS) * g_ref[...]
        hb_ref[...] = y.astype(hb_ref.dtype)
        h_ref[...] = y.astype(h_ref.dtype)

    def emit(c, s0, s1, rank0, a_top, rank1, b_top):
        n, best = _young_counts(a_top, b_top)
        z = jnp.zeros_like(best[0])
        for m in best:
            z = z + jnp.exp(m - best[0])
        n0 = jnp.zeros_like(s0)
        for a in range(PEER_TOPK):
            n0 = jnp.where(rank0 == float(a), n[a:a + 1, :], n0)
        sl = slice(c * LANES, (c + 1) * LANES)
        e1 = jnp.where(rank1 < NOT_TOP, jnp.exp(s1 - b_top[0]), 0.0)
        r1_ref[:, sl] = pltpu.bitcast(rank1.astype(GATE_DTYPE), r1_ref.dtype)
        e1_ref[:, sl] = pltpu.bitcast(e1.astype(GATE_DTYPE), e1_ref.dtype)
        n0_ref[:, sl] = n0
        c0_ref[:, sl] = jnp.exp(s0 - a_top[0]) * (0.5 / z)

    chunks = range(SEL_TOKENS // LANES)
    scores = []
    qs = [jnp.dot(hb_ref[c * LANES:(c + 1) * LANES, :], wq_ref[...],
                  preferred_element_type=jnp.float32).astype(keys_ref.dtype) for c in chunks]
    for qc in qs:
        scores.append((_dot_nt(keys_ref[0, 0], qc[:, :PEER_HALF]),
                       _dot_nt(keys_ref[0, 1], qc[:, PEER_HALF:])))
    untied = [(_topk_rank_untied(s0), _topk_rank_untied(s1)) for s0, s1 in scores]
    removed = functools.reduce(jnp.maximum, [t[2] for pair in untied for t in pair])
    tied = jnp.max(removed) > float(PEER_TOPK)

    @pl.when(tied)
    def _():
        for c, (s0, s1) in zip(chunks, scores):
            emit(c, s0, s1, *_topk_rank(s0), *_topk_rank(s1))

    @pl.when(jnp.logical_not(tied))
    def _():
        for c, (s0, s1), (t0, t1) in zip(chunks, scores, untied):
            emit(c, s0, s1, t0[0], t0[1], t1[0], t1[1])


def _twice_gelu_tanh(x):
    c = math.sqrt(2.0 / math.pi)
    return x * (1.0 + jnp.tanh(x * (c + (c * 0.044715) * (x * x))))


def _peer_mix_kernel(h_ref, u_ref, vt_ref, r1_ref, e1_ref, n0_ref, c0_ref, x_ref, fg_ref, y_ref,
                     acc_ref, a_ref, w_ref):
    e = pl.program_id(1)

    @pl.when(e == 0)
    def _():
        acc_ref[...] = jnp.zeros_like(acc_ref)

    gd = GATE_DTYPE
    pack = SUBLANES * GATE_PACK
    krows = N_KEYS // GATE_PACK
    half = MIX_TOKENS // 2
    for hf in range(2):
        a_ref[:, hf * half:(hf + 1) * half] = _dot_nt(u_ref[...], h_ref[hf * half:(hf + 1) * half, :])
    rows = [pl.ds(pl.multiple_of(hd * N_KEYS + e * MIX_KEYS, MIX_KEYS), MIX_KEYS) for hd in range(PEER_HEADS)]
    for hf in range(2):
        for c in range(hf * half // LANES, (hf + 1) * half // LANES):
            sl = slice(c * LANES, (c + 1) * LANES)
            n_t = [n0_ref[r, sl] for r in rows]
            c_t = [c0_ref[r, sl] for r in rows]
            for ii in range(MIX_KEYS):
                g = jnp.zeros((N_KEYS // pack, pack, LANES), gd)
                for hd in range(PEER_HEADS):
                    n_b = jnp.broadcast_to(n_t[hd][ii:ii + 1, :], (pack, LANES)).astype(gd)[None]
                    c_b = jnp.broadcast_to(c_t[hd][ii:ii + 1, :], (pack, LANES)).astype(gd)[None]
                    r1 = pltpu.bitcast(r1_ref[hd * krows:(hd + 1) * krows, sl], gd).reshape(g.shape)
                    e1 = pltpu.bitcast(e1_ref[hd * krows:(hd + 1) * krows, sl], gd).reshape(g.shape)
                    g = g + jnp.where(r1 < n_b, e1, jnp.zeros_like(e1)) * c_b
                act = _twice_gelu_tanh(a_ref[ii * N_KEYS:(ii + 1) * N_KEYS, sl])
                gate = g.reshape(N_KEYS, LANES).astype(jnp.float32)
                w_ref[ii * N_KEYS:(ii + 1) * N_KEYS, sl] = (act * gate).astype(w_ref.dtype)
        hs = slice(hf * half, (hf + 1) * half)
        acc_ref[:, hs] += jnp.dot(vt_ref[...], w_ref[:, hs], preferred_element_type=jnp.float32)

    @pl.when(e == pl.num_programs(1) - 1)
    def _():
        o = acc_ref[...].T + x_ref[...]
        y_ref[...] = o * lax.rsqrt(jnp.mean(o * o, axis=-1, keepdims=True) + EPS) * fg_ref[...]


def peer_and_final_norm(x, norm2_g, w_q, sub_keys, u, v, final_g):
    T, D = x.shape
    assert T % MIX_TOKENS == 0 and T % SEL_TOKENS == 0 and N_KEYS % MIX_KEYS == 0
    f32 = jnp.float32
    nq = PEER_HEADS * PEER_QDIM
    tab = jax.ShapeDtypeStruct((PEER_HEADS * N_KEYS, T), f32)
    gtab = jax.ShapeDtypeStruct((PEER_HEADS * N_KEYS // GATE_PACK, T), jnp.uint32)
    tab_spec = pl.BlockSpec((N_KEYS, SEL_TOKENS), lambda t, h: (h, t))
    gtab_spec = pl.BlockSpec((N_KEYS // GATE_PACK, SEL_TOKENS), lambda t, h: (h, t))
    h, r1, e1, n0, c0 = pl.pallas_call(
        _peer_select_kernel,
        out_shape=(jax.ShapeDtypeStruct((T, D), MM_DTYPE), gtab, gtab, tab, tab),
        grid=(T // SEL_TOKENS, PEER_HEADS),
        in_specs=[pl.BlockSpec((SEL_TOKENS, D), lambda t, h: (t, 0)),
                  pl.BlockSpec((1, D), lambda t, h: (0, 0)),
                  pl.BlockSpec((D, PEER_QDIM), lambda t, h: (0, h)),
                  pl.BlockSpec((1, 2, N_KEYS, PEER_HALF), lambda t, h: (h, 0, 0, 0))],
        out_specs=(pl.BlockSpec((SEL_TOKENS, D), lambda t, h: (t, 0)), gtab_spec, gtab_spec, tab_spec, tab_spec),
        scratch_shapes=[pltpu.VMEM((SEL_TOKENS, D), MM_DTYPE)],
        compiler_params=pltpu.CompilerParams(dimension_semantics=("parallel", "arbitrary"),
                                             vmem_limit_bytes=VMEM_LIMIT),
        name="peer_select",
    )(x, norm2_g.reshape(1, D), w_q.astype(MM_DTYPE), sub_keys.astype(MM_DTYPE))

    eb = MIX_KEYS * N_KEYS
    mix_tab = pl.BlockSpec((PEER_HEADS * N_KEYS, MIX_TOKENS), lambda t, e: (0, t))
    mix_gtab = pl.BlockSpec((PEER_HEADS * N_KEYS // GATE_PACK, MIX_TOKENS), lambda t, e: (0, t))
    tok_blk = pl.BlockSpec((MIX_TOKENS, D), lambda t, e: (t, 0))
    return pl.pallas_call(
        _peer_mix_kernel,
        out_shape=jax.ShapeDtypeStruct((T, D), f32),
        grid=(T // MIX_TOKENS, N_KEYS // MIX_KEYS),
        in_specs=[tok_blk,
                  pl.BlockSpec((eb, D), lambda t, e: (e, 0)),
                  pl.BlockSpec((D, eb), lambda t, e: (0, e)),
                  mix_gtab, mix_gtab, mix_tab, mix_tab,
                  tok_blk,
                  pl.BlockSpec((1, D), lambda t, e: (0, 0))],
        out_specs=tok_blk,
        scratch_shapes=[pltpu.VMEM((D, MIX_TOKENS), f32),
                        pltpu.VMEM((eb, MIX_TOKENS), f32),
                        pltpu.VMEM((eb, MIX_TOKENS), MM_DTYPE)],
        compiler_params=pltpu.CompilerParams(dimension_semantics=("parallel", "arbitrary"),
                                             vmem_limit_bytes=VMEM_LIMIT),
        name="peer_mix",
    )(h, u.astype(MM_DTYPE), v.T.astype(MM_DTYPE), r1, e1, n0, c0, x, final_g.reshape(1, D))


NA_BAND = NA_ROWS * GRID_W
NA_MASKED = -1e30


def _na_bias_table(rpb):
    cq = jnp.arange(GRID_W)[:, None]
    ck = jnp.arange(GRID_W)[None, :]
    cs = jnp.clip(cq - NA_COLS // 2, 0, GRID_W - NA_COLS)
    valid = (ck >= cs) & (ck < cs + NA_COLS)
    cb = jnp.clip(ck - cq + NA_COLS - 1, 0, 2 * NA_COLS - 2)
    rb = jnp.arange(NA_ROWS)[:, None] + jnp.arange(NA_ROWS)[None, :]
    tab = rpb[:, rb][:, :, :, cb]
    tab = jnp.where(valid, tab, NA_MASKED)
    return jnp.transpose(tab, (1, 0, 3, 2, 4)).reshape(NA_ROWS, NA_HEADS, GRID_W, NA_BAND)


def _na_row_start(r, rows):
    return jnp.clip(r - NA_ROWS // 2, 0, rows - NA_ROWS)


def _na_kernel(q_ref, k_ref, v_ref, tab_ref, o_ref, *, rows):
    start = pl.multiple_of(_na_row_start(pl.program_id(1), rows) * GRID_W, GRID_W)
    kb = k_ref[0, pl.ds(start, NA_BAND), :]
    vb = v_ref[0, pl.ds(start, NA_BAND), :]
    q = q_ref[0]
    lane = lax.broadcasted_iota(jnp.int32, (GRID_W, LANES), 1)
    lo = lane < NA_HEAD_DIM
    pairs = [slice(hp * LANES, (hp + 1) * LANES) for hp in range(NA_HEADS // 2)]
    scores = []
    for h in range(NA_HEADS):
        qp = q[:, pairs[h // 2]]
        mine = lo if h % 2 == 0 else jnp.logical_not(lo)
        scores.append(_dot_nt(jnp.where(mine, qp, jnp.zeros_like(qp)), kb[:, pairs[h // 2]]) + tab_ref[0, h])
    probs, sums = [], []
    for sc in scores:
        p = jnp.exp(sc - jnp.max(sc, axis=-1, keepdims=True))
        sums.append(jnp.sum(p, axis=-1, keepdims=True))
        probs.append(p.astype(vb.dtype))
    outs = []
    for hp in range(NA_HEADS // 2):
        o = [jnp.dot(probs[2 * hp + s], vb[:, pairs[hp]], preferred_element_type=jnp.float32) / sums[2 * hp + s]
             for s in range(2)]
        outs.append(jnp.where(lo, o[0], o[1]))
    o_ref[0] = jnp.concatenate(outs, axis=-1).astype(o_ref.dtype)


def neighbourhood_attention_pallas(q, k, v, rpb):
    B, L, W = q.shape
    rows = L // GRID_W
    assert rows >= NA_ROWS and W == NA_HEADS * NA_HEAD_DIM and 2 * NA_HEAD_DIM == LANES
    q = (q * NA_HEAD_DIM ** -0.5).astype(MM_DTYPE)
    seq = pl.BlockSpec((1, L, W), lambda b, r: (b, 0, 0))
    row = pl.BlockSpec((1, GRID_W, W), lambda b, r: (b, r, 0))
    tab = pl.BlockSpec((1, NA_HEADS, GRID_W, NA_BAND),
                       lambda b, r: (_na_row_start(r, rows) - r + NA_ROWS - 1, 0, 0, 0))
    return pl.pallas_call(
        functools.partial(_na_kernel, rows=rows),
        out_shape=jax.ShapeDtypeStruct((B, L, W), jnp.float32),
        grid=(B, rows),
        in_specs=[row, seq, seq, tab],
        out_specs=row,
        compiler_params=pltpu.CompilerParams(dimension_semantics=("parallel", "arbitrary"),
                                             vmem_limit_bytes=VMEM_LIMIT),
        name="na_attention",
    )(q, k.astype(MM_DTYPE), v.astype(MM_DTYPE), _na_bias_table(rpb))


SUBLANES = 8


def _rotate_half(x, cos, sin_signed):
    return x * cos + pltpu.roll(x, x.shape[-1] // 2, axis=1) * sin_signed


def _dot_tn(a, b):
    return lax.dot_general(a, b, (((0,), (0,)), ((), ())), preferred_element_type=jnp.float32)


def _ret_kernel(q_ref, k_ref, v_ref, g_ref, cos_ref, sin_ref, df_ref, db_ref, ng_ref, o_ref,
                of_ref, ob_ref, *, nc):
    C = RET_CHUNK
    mm = o_ref.dtype
    lg_f = jax.nn.log_sigmoid(df_ref[0])[0:1, :]
    lg_b = jax.nn.log_sigmoid(db_ref[0])[0:1, :]
    pos = lax.broadcasted_iota(jnp.int32, (C, C), 0).astype(jnp.float32)
    diff = pos - lax.broadcasted_iota(jnp.int32, (C, C), 1).astype(jnp.float32)
    dec_f = jnp.where(diff >= 0, jnp.exp(lg_f * jnp.maximum(diff, 0.0)), 0.0)
    dec_b = jnp.where(diff < 0, jnp.exp(lg_b * jnp.maximum(-diff, 0.0)), 0.0)
    kdec_f = jnp.exp(lg_f * (C - 1 - pos))
    qdec_f = jnp.exp(lg_f * (pos + 1))
    kdec_b = jnp.exp(lg_b * pos)
    qdec_b = jnp.exp(lg_b * (C - pos))
    step_f = jnp.exp(lg_f * C)
    step_b = jnp.exp(lg_b * C)
    scale = RET_HEAD_DIM ** -0.5

    def load(c):
        sl = pl.ds(pl.multiple_of(c * C, C), C)
        q = _rotate_half(q_ref[0, sl, :], cos_ref[sl, :], sin_ref[sl, :])
        k = _rotate_half(k_ref[0, sl, :], cos_ref[sl, :], sin_ref[sl, :]) * scale
        return sl, q, k, v_ref[0, sl, :].astype(mm)

    def body(c, carry):
        rf, rb = carry
        sl_f, q_f, k_f, v_f = load(c)
        sl_b, q_b, k_b, v_b = load(nc - 1 - c)
        att_f = _dot_nt(q_f.astype(mm), k_f.astype(mm))
        att_b = _dot_nt(q_b.astype(mm), k_b.astype(mm))
        inter_f = jnp.dot((q_f * qdec_f).astype(mm), rf.astype(mm), preferred_element_type=jnp.float32)
        inter_b = jnp.dot((q_b * qdec_b).astype(mm), rb.astype(mm), preferred_element_type=jnp.float32)
        s_f = _dot_tn((k_f * kdec_f).astype(mm), v_f)
        s_b = _dot_tn((k_b * kdec_b).astype(mm), v_b)
        of_ref[sl_f, :] = inter_f + jnp.dot((att_f * dec_f).astype(mm), v_f, preferred_element_type=jnp.float32)
        ob_ref[sl_b, :] = inter_b + jnp.dot((att_b * dec_b).astype(mm), v_b, preferred_element_type=jnp.float32)
        return step_f * rf + s_f, step_b * rb + s_b

    zero = jnp.zeros((RET_HEAD_DIM, RET_HEAD_DIM), jnp.float32)
    lax.fori_loop(0, nc, body, (zero, zero))

    def finish(c, carry):
        sl = pl.ds(pl.multiple_of(c * C, C), C)
        o = of_ref[sl, :] + ob_ref[sl, :]
        mu = jnp.mean(o, axis=-1, keepdims=True)
        var = jnp.mean(jnp.square(o - mu), axis=-1, keepdims=True)
        o = (o - mu) * lax.rsqrt(var + EPS) * ng_ref[...]
        o_ref[0, sl, :] = (jax.nn.silu(g_ref[0, sl, :]) * o).astype(mm)
        return carry

    lax.fori_loop(0, nc, finish, 0)


def retention_pallas(q, k, v, gate, decay_fwd, decay_bwd, norm_g):
    B, L, W = q.shape
    dh = RET_HEAD_DIM
    assert W == RET_HEADS * dh and dh == LANES and L % RET_CHUNK == 0
    inv = 1.0 / (ROPE_BASE ** (jnp.arange(0, dh, 2, dtype=jnp.float32) / dh))
    ang = jnp.arange(L, dtype=jnp.float32)[:, None] * inv[None, :]
    cos = jnp.concatenate([jnp.cos(ang), jnp.cos(ang)], axis=-1)
    sin = jnp.concatenate([-jnp.sin(ang), jnp.sin(ang)], axis=-1)
    bcast = lambda d: jnp.broadcast_to(d.astype(jnp.float32)[:, None, None], (RET_HEADS, SUBLANES, dh))
    seq = pl.BlockSpec((1, L, dh), lambda b, h: (b, 0, h))
    tab = pl.BlockSpec((L, dh), lambda b, h: (0, 0))
    dec = pl.BlockSpec((1, SUBLANES, dh), lambda b, h: (h, 0, 0))
    return pl.pallas_call(
        functools.partial(_ret_kernel, nc=L // RET_CHUNK),
        out_shape=jax.ShapeDtypeStruct((B, L, W), MM_DTYPE),
        grid=(B, RET_HEADS),
        in_specs=[seq, seq, seq, seq, tab, tab, dec, dec, pl.BlockSpec((1, dh), lambda b, h: (0, h))],
        out_specs=seq,
        scratch_shapes=[pltpu.VMEM((L, dh), jnp.float32), pltpu.VMEM((L, dh), jnp.float32)],
        compiler_params=pltpu.CompilerParams(dimension_semantics=("parallel", "parallel"),
                                             vmem_limit_bytes=VMEM_LIMIT),
        name="retention",
    )(q, k, v, gate, cos, sin, bcast(decay_fwd), bcast(decay_bwd), norm_g.reshape(1, W))


PROJ_TOKENS = 512
N_NA_STREAMS = 3
N_RET_STREAMS = 4


def _in_proj_kernel(x_ref, g_ref, w_ref, *out_refs):
    x = x_ref[...]
    h = (x * lax.rsqrt(jnp.mean(x * x, axis=-1, keepdims=True) + EPS) * g_ref[...]).astype(w_ref.dtype)
    for j, o_ref in enumerate(out_refs):
        o_ref[...] = jnp.dot(h, w_ref[:, j * NA_WIDTH:(j + 1) * NA_WIDTH],
                             preferred_element_type=jnp.float32).astype(o_ref.dtype)


def in_projection(x, norm_g, w_in):
    T, D = x.shape
    assert NA_WIDTH == RET_WIDTH and T % PROJ_TOKENS == 0
    n = N_NA_STREAMS + N_RET_STREAMS
    assert w_in.shape == (D, n * NA_WIDTH)
    tok = pl.BlockSpec((PROJ_TOKENS, D), lambda t: (t, 0))
    out = pl.BlockSpec((PROJ_TOKENS, NA_WIDTH), lambda t: (t, 0))
    return pl.pallas_call(
        _in_proj_kernel,
        out_shape=tuple(jax.ShapeDtypeStruct((T, NA_WIDTH), jnp.float32) for _ in range(n)),
        grid=(T // PROJ_TOKENS,),
        in_specs=[tok, pl.BlockSpec((1, D), lambda t: (0, 0)), pl.BlockSpec(w_in.shape, lambda t: (0, 0))],
        out_specs=tuple(out for _ in range(n)),
        compiler_params=pltpu.CompilerParams(dimension_semantics=("parallel",), vmem_limit_bytes=VMEM_LIMIT),
        name="in_projection",
    )(x, norm_g.reshape(1, D), w_in.astype(MM_DTYPE))


def _out_proj_kernel(a_ref, r_ref, w_ref, x_ref, o_ref):
    o = jnp.dot(a_ref[...].astype(w_ref.dtype), w_ref[:NA_WIDTH, :], preferred_element_type=jnp.float32)
    o = o + jnp.dot(r_ref[...].astype(w_ref.dtype), w_ref[NA_WIDTH:, :], preferred_element_type=jnp.float32)
    o_ref[...] = x_ref[...] + o


def out_projection(na_o, ret_o, w_out, x):
    T, D = x.shape
    tok = pl.BlockSpec((PROJ_TOKENS, D), lambda t: (t, 0))
    half = pl.BlockSpec((PROJ_TOKENS, NA_WIDTH), lambda t: (t, 0))
    return pl.pallas_call(
        _out_proj_kernel,
        out_shape=jax.ShapeDtypeStruct((T, D), jnp.float32),
        grid=(T // PROJ_TOKENS,),
        in_specs=[half, half, pl.BlockSpec(w_out.shape, lambda t: (0, 0)), tok],
        out_specs=tok,
        compiler_params=pltpu.CompilerParams(dimension_semantics=("parallel",), vmem_limit_bytes=VMEM_LIMIT),
        name="out_projection",
    )(na_o, ret_o, w_out.astype(MM_DTYPE), x)


def encoder_layer(x, norm1_g, w_in, na_rpb, ret_decay_fwd, ret_decay_bwd, ret_norm_g,
                  w_out, norm2_g, peer_w_q, peer_sub_keys, peer_u, peer_v, final_g):
    B, L, D = x.shape
    xt = x.reshape(B * L, D)
    streams = [s.reshape(B, L, NA_WIDTH) for s in in_projection(xt, norm1_g, w_in)]
    na_o = neighbourhood_attention_pallas(*streams[:N_NA_STREAMS], na_rpb)
    ret_o = retention_pallas(*streams[N_NA_STREAMS:], ret_decay_fwd, ret_decay_bwd, ret_norm_g)
    x1 = out_projection(na_o.reshape(B * L, NA_WIDTH), ret_o.reshape(B * L, RET_WIDTH), w_out, xt)
    y = peer_and_final_norm(x1, norm2_g, peer_w_q, peer_sub_keys, peer_u, peer_v, final_g)
    return y.reshape(B, L, D)


def kernel(x_prompt, x_sample, norm1_g, w_in, na_rpb, ret_decay_fwd, ret_decay_bwd, ret_norm_g,
           w_out, norm2_g, peer_w_q, peer_sub_keys, peer_u, peer_v, final_g):
    assert DEPTH == 1

    def trunk(x):
        return encoder_layer(x, norm1_g[0], w_in[0], na_rpb[0], ret_decay_fwd[0], ret_decay_bwd[0],
                             ret_norm_g[0], w_out[0], norm2_g[0], peer_w_q[0], peer_sub_keys[0],
                             peer_u[0], peer_v[0], final_g)

    return (trunk(x_prompt), trunk(x_sample))
```

```python
import functools
import math
import jax, jax.numpy as jnp
from jax import lax
import numpy as np
from jax.experimental import pallas as pl
from jax.experimental.pallas import tpu as pltpu

D_MODEL = 1024
DEPTH = 1
GRID_W = 64
NA_HEADS = 8
NA_HEAD_DIM = 64
NA_WIDTH = NA_HEADS * NA_HEAD_DIM
NA_ROWS = 8
NA_COLS = 16
RET_HEADS = 4
RET_HEAD_DIM = 128
RET_WIDTH = RET_HEADS * RET_HEAD_DIM
RET_CHUNK = 128
ROPE_BASE = 10000.0
PEER_HEADS = 8
PEER_QDIM = 256
PEER_HALF = PEER_QDIM // 2
N_KEYS = 128
N_EXPERTS = N_KEYS * N_KEYS
PEER_TOPK = 16
PEER_BLOCK = 128
EPS = 1e-6


def rmsnorm(x, g):
    xf = x.astype(jnp.float32)
    y = xf * lax.rsqrt(jnp.mean(xf * xf, axis=-1, keepdims=True) + EPS)
    return (y * g.astype(jnp.float32)).astype(x.dtype)


def rotary(x):
    L, d = x.shape[1], x.shape[-1]
    inv = 1.0 / (ROPE_BASE ** (jnp.arange(0, d, 2, dtype=jnp.float32) / d))
    ang = jnp.arange(L, dtype=jnp.float32)[:, None] * inv[None, :]
    cos = jnp.cos(ang)[None, :, None, :]
    sin = jnp.sin(ang)[None, :, None, :]
    x1 = x[..., : d // 2].astype(jnp.float32)
    x2 = x[..., d // 2:].astype(jnp.float32)
    return jnp.concatenate([x1 * cos - x2 * sin, x1 * sin + x2 * cos], axis=-1).astype(x.dtype)


def neighbourhood_attention(q, k, v, rpb):
    B, L, H, dh = q.shape
    rows = L // GRID_W
    kr = min(NA_ROWS, rows)
    qg = q.reshape(B, rows, GRID_W, H, dh)
    kg = k.reshape(B, rows, GRID_W, H, dh)
    vg = v.reshape(B, rows, GRID_W, H, dh)
    r_idx = jnp.arange(rows)
    row_start = jnp.clip(r_idx - kr // 2, 0, rows - kr)
    c_idx = jnp.arange(GRID_W)
    col_start = jnp.clip(c_idx - NA_COLS // 2, 0, GRID_W - NA_COLS)
    col_idx = col_start[:, None] + jnp.arange(NA_COLS)[None, :]
    col_bias_idx = col_idx - c_idx[:, None] + (NA_COLS - 1)
    scale = dh ** -0.5

    def one_row(args):
        qr, rs, ri = args
        k_band = lax.dynamic_slice_in_dim(kg, rs, kr, axis=1)
        v_band = lax.dynamic_slice_in_dim(vg, rs, kr, axis=1)
        k_nb = k_band[:, :, col_idx]
        v_nb = v_band[:, :, col_idx]
        s = jnp.einsum('bqhd,brqjhd->bhqrj', qr, k_nb).astype(jnp.float32) * scale
        row_bias_idx = rs + jnp.arange(kr) - ri + (NA_ROWS - 1)
        bias = rpb[:, row_bias_idx][:, :, col_bias_idx]
        s = s + jnp.transpose(bias, (0, 2, 1, 3)).astype(jnp.float32)[None]
        p = jax.nn.softmax(s.reshape(B, H, GRID_W, kr * NA_COLS), axis=-1)
        p = p.reshape(B, H, GRID_W, kr, NA_COLS).astype(v.dtype)
        return jnp.einsum('bhqrj,brqjhd->bqhd', p, v_nb)

    out = lax.map(one_row, (jnp.transpose(qg, (1, 0, 2, 3, 4)), row_start, r_idx))
    return jnp.transpose(out, (1, 0, 2, 3, 4)).reshape(B, L, H * dh)


def chunk_retention(q, k, v, log_gamma, strict):
    B, L, H, dk = q.shape
    dv = v.shape[-1]
    C = RET_CHUNK
    nc = L // C
    dt = q.dtype
    qc = q.reshape(B, nc, C, H, dk)
    kc = k.reshape(B, nc, C, H, dk)
    vc = v.reshape(B, nc, C, H, dv)
    pos = jnp.arange(C, dtype=jnp.float32)
    diff = pos[:, None] - pos[None, :]
    mask = (diff > 0) if strict else (diff >= 0)
    decay = jnp.where(mask[None], jnp.exp(log_gamma[:, None, None] * jnp.maximum(diff, 0.0)[None]), 0.0).astype(dt)
    k_decay = jnp.exp(log_gamma[None, :] * (C - 1 - pos)[:, None]).astype(dt)
    q_decay = jnp.exp(log_gamma[None, :] * (pos + 1)[:, None]).astype(dt)
    chunk_decay = jnp.exp(log_gamma * C).astype(dt)
    att = jnp.einsum('bnqhd,bnkhd->bnhqk', qc, kc) * decay[None, None]
    o_intra = jnp.einsum('bnhqk,bnkhe->bnqhe', att, vc)
    S = jnp.einsum('bnkhd,kh,bnkhe->nbhde', kc, k_decay, vc)

    def step(R, S_i):
        return chunk_decay[None, :, None, None] * R + S_i, R

    _, R_prev = lax.scan(step, jnp.zeros(S.shape[1:], S.dtype), S)
    o_inter = jnp.einsum('bnqhd,qh,nbhde->bnqhe', qc, q_decay, R_prev)
    return (o_intra + o_inter).reshape(B, L, H, dv)


def bidirectional_retention(q, k, v, gate, decay_fwd, decay_bwd, norm_g):
    B, L, H, dk = q.shape
    q = rotary(q)
    k = rotary(k) * (dk ** -0.5)
    lg_f = jax.nn.log_sigmoid(decay_fwd.astype(jnp.float32))
    lg_b = jax.nn.log_sigmoid(decay_bwd.astype(jnp.float32))
    o_f = chunk_retention(q, k, v, lg_f, False)
    o_b = jnp.flip(chunk_retention(jnp.flip(q, 1), jnp.flip(k, 1), jnp.flip(v, 1), lg_b, True), 1)
    o = (o_f + o_b).astype(jnp.float32)
    mu = jnp.mean(o, axis=-1, keepdims=True)
    var = jnp.mean(jnp.square(o - mu), axis=-1, keepdims=True)
    o = (o - mu) * lax.rsqrt(var + EPS)
    o = o.reshape(B, L, H * v.shape[-1]) * norm_g.astype(jnp.float32)
    return (jax.nn.silu(gate.astype(jnp.float32)) * o).astype(v.dtype)


MM_DTYPE = jnp.bfloat16
GATE_DTYPE = jnp.bfloat16
GATE_PACK = 2
LANES = 128
NOT_TOP = 255.0
SEL_TOKENS = 512
MIX_TOKENS = 512
MIX_KEYS = 16
VMEM_LIMIT = 56 * 1024 * 1024


def _dot_nt(a, b):
    return lax.dot_general(a, b, (((1,), (1,)), ((), ())), preferred_element_type=jnp.float32)


def _topk_rank(s):
    iota = lax.broadcasted_iota(jnp.int32, s.shape, 0).astype(jnp.float32)
    rank = jnp.full(s.shape, NOT_TOP, jnp.float32)
    tops = []
    for r in range(PEER_TOPK):
        m = jnp.max(s, axis=0, keepdims=True)
        idx = jnp.min(jnp.where(s == m, iota, float(s.shape[0])), axis=0, keepdims=True)
        hit = iota == idx
        rank = jnp.where(hit, float(r), rank)
        s = jnp.where(hit, -jnp.inf, s)
        tops.append(m)
    return rank, tops


def _topk_rank_untied(s):
    rank = jnp.full(s.shape, NOT_TOP, jnp.float32)
    tops = []
    for r in range(PEER_TOPK):
        m = jnp.max(s, axis=0, keepdims=True)
        hit = s == m
        rank = jnp.where(hit, float(r), rank)
        s = jnp.where(hit, -jnp.inf, s)
        tops.append(m)
    removed = jnp.sum(jnp.where(rank < NOT_TOP, 1.0, 0.0), axis=0, keepdims=True)
    return rank, tops, removed


def _young_counts(a_top, b_top):
    k = PEER_TOPK
    w = a_top[0].shape[1]
    b16 = jnp.concatenate(b_top, axis=0)
    b8 = jnp.concatenate(b_top[:8], axis=0)
    blocks = [a_top[0] + b16] + [a_top[a] + b8 for a in range(1, 8)]
    blocks.append(jnp.concatenate(a_top[8:], axis=0) + b_top[0])
    cand = jnp.concatenate(blocks, axis=0)
    rows = cand.shape[0]
    rho = lax.broadcasted_iota(jnp.int32, (rows, w), 0)
    mid = rho - 16
    a_id = jnp.where(rho < 16, 0, jnp.where(rho < 72, 1 + (mid >> 3), rho - 64))
    b_id = jnp.where(rho < 16, rho, jnp.where(rho < 72, mid & 7, 0))
    pos = (a_id * k + b_id).astype(jnp.float32)
    cand = jnp.where((a_id + 1) * (b_id + 1) <= k, cand, -jnp.inf)
    a_iota = lax.broadcasted_iota(jnp.int32, (k, w), 0).astype(jnp.float32)
    n = jnp.zeros((k, w), jnp.float32)
    best = []
    for _ in range(k):
        m = jnp.max(cand, axis=0, keepdims=True)
        p = jnp.min(jnp.where(cand == m, pos, float(k * k)), axis=0, keepdims=True)
        cand = jnp.where(pos == p, -jnp.inf, cand)
        n = n + (a_iota == jnp.floor(p * (1.0 / k))).astype(jnp.float32)
        best.append(m)
    return n, best


def _peer_select_kernel(x_ref, g_ref, wq_ref, keys_ref, h_ref, r1_ref, e1_ref, n0_ref, c0_ref, hb_ref):
    hd = pl.program_id(1)

    @pl.when(hd == 0)
    def _():
        x = x_ref[...]
        y = x * lax.rsqrt(jnp.mean(x * x, axis=-1, keepdims=True) + EPS) * g_ref[...]
        hb_ref[...] = y.astype(hb_ref.dtype)
        h_ref[...] = y.astype(h_ref.dtype)

    def emit(c, s0, s1, rank0, a_top, rank1, b_top):
        n, best = _young_counts(a_top, b_top)
        z = jnp.zeros_like(best[0])
        for m in best:
            z = z + jnp.exp(m - best[0])
        n0 = jnp.zeros_like(s0)
        for a in range(PEER_TOPK):
            n0 = jnp.where(rank0 == float(a), n[a:a + 1, :], n0)
        sl = slice(c * LANES, (c + 1) * LANES)
        e1 = jnp.where(rank1 < NOT_TOP, jnp.exp(s1 - b_top[0]), 0.0)
        r1_ref[:, sl] = pltpu.bitcast(rank1.astype(GATE_DTYPE), r1_ref.dtype)
        e1_ref[:, sl] = pltpu.bitcast(e1.astype(GATE_DTYPE), e1_ref.dtype)
        n0_ref[:, sl] = n0
        c0_ref[:, sl] = jnp.exp(s0 - a_top[0]) * (0.5 / z)

    chunks = range(SEL_TOKENS // LANES)
    scores = []
    qs = [jnp.dot(hb_ref[c * LANES:(c + 1) * LANES, :], wq_ref[...],
                  preferred_element_type=jnp.float32).astype(keys_ref.dtype) for c in chunks]
    for qc in qs:
        scores.append((_dot_nt(keys_ref[0, 0], qc[:, :PEER_HALF]),
                       _dot_nt(keys_ref[0, 1], qc[:, PEER_HALF:])))
    untied = [(_topk_rank_untied(s0), _topk_rank_untied(s1)) for s0, s1 in scores]
    removed = functools.reduce(jnp.maximum, [t[2] for pair in untied for t in pair])
    tied = jnp.max(removed) > float(PEER_TOPK)

    @pl.when(tied)
    def _():
        for c, (s0, s1) in zip(chunks, scores):
            emit(c, s0, s1, *_topk_rank(s0), *_topk_rank(s1))

    @pl.when(jnp.logical_not(tied))
    def _():
        for c, (s0, s1), (t0, t1) in zip(chunks, scores, untied):
            emit(c, s0, s1, t0[0], t0[1], t1[0], t1[1])


def _twice_gelu_tanh(x):
    c = math.sqrt(2.0 / math.pi)
    return x * (1.0 + jnp.tanh(x * (c + (c * 0.044715) * (x * x))))


def _peer_mix_kernel(h_ref, u_ref, vt_ref, r1_ref, e1_ref, n0_ref, c0_ref, x_ref, fg_ref, y_ref,
                     acc_ref, a_ref, w_ref):
    e = pl.program_id(1)

    @pl.when(e == 0)
    def _():
        acc_ref[...] = jnp.zeros_like(acc_ref)

    gd = GATE_DTYPE
    pack = SUBLANES * GATE_PACK
    krows = N_KEYS // GATE_PACK
    half = MIX_TOKENS // 2
    for hf in range(2):
        a_ref[:, hf * half:(hf + 1) * half] = _dot_nt(u_ref[...], h_ref[hf * half:(hf + 1) * half, :])
    rows = [pl.ds(pl.multiple_of(hd * N_KEYS + e * MIX_KEYS, MIX_KEYS), MIX_KEYS) for hd in range(PEER_HEADS)]
    for hf in range(2):
        for c in range(hf * half // LANES, (hf + 1) * half // LANES):
            sl = slice(c * LANES, (c + 1) * LANES)
            n_t = [n0_ref[r, sl] for r in rows]
            c_t = [c0_ref[r, sl] for r in rows]
            for ii in range(MIX_KEYS):
                g = jnp.zeros((N_KEYS // pack, pack, LANES), gd)
                for hd in range(PEER_HEADS):
                    n_b = jnp.broadcast_to(n_t[hd][ii:ii + 1, :], (pack, LANES)).astype(gd)[None]
                    c_b = jnp.broadcast_to(c_t[hd][ii:ii + 1, :], (pack, LANES)).astype(gd)[None]
                    r1 = pltpu.bitcast(r1_ref[hd * krows:(hd + 1) * krows, sl], gd).reshape(g.shape)
                    e1 = pltpu.bitcast(e1_ref[hd * krows:(hd + 1) * krows, sl], gd).reshape(g.shape)
                    g = g + jnp.where(r1 < n_b, e1, jnp.zeros_like(e1)) * c_b
                act = _twice_gelu_tanh(a_ref[ii * N_KEYS:(ii + 1) * N_KEYS, sl])
                gate = g.reshape(N_KEYS, LANES).astype(jnp.float32)
                w_ref[ii * N_KEYS:(ii + 1) * N_KEYS, sl] = (act * gate).astype(w_ref.dtype)
        hs = slice(hf * half, (hf + 1) * half)
        acc_ref[:, hs] += jnp.dot(vt_ref[0], w_ref[:, hs], preferred_element_type=jnp.float32)

    @pl.when(e == pl.num_programs(1) - 1)
    def _():
        o = acc_ref[...].T + x_ref[...]
        y_ref[...] = o * lax.rsqrt(jnp.mean(o * o, axis=-1, keepdims=True) + EPS) * fg_ref[...]


def peer_and_final_norm(x, norm2_g, w_q, sub_keys, u, v, final_g):
    T, D = x.shape
    assert T % MIX_TOKENS == 0 and T % SEL_TOKENS == 0 and N_KEYS % MIX_KEYS == 0
    f32 = jnp.float32
    nq = PEER_HEADS * PEER_QDIM
    tab = jax.ShapeDtypeStruct((PEER_HEADS * N_KEYS, T), f32)
    gtab = jax.ShapeDtypeStruct((PEER_HEADS * N_KEYS // GATE_PACK, T), jnp.uint32)
    tab_spec = pl.BlockSpec((N_KEYS, SEL_TOKENS), lambda t, h: (h, t))
    gtab_spec = pl.BlockSpec((N_KEYS // GATE_PACK, SEL_TOKENS), lambda t, h: (h, t))
    h, r1, e1, n0, c0 = pl.pallas_call(
        _peer_select_kernel,
        out_shape=(jax.ShapeDtypeStruct((T, D), MM_DTYPE), gtab, gtab, tab, tab),
        grid=(T // SEL_TOKENS, PEER_HEADS),
        in_specs=[pl.BlockSpec((SEL_TOKENS, D), lambda t, h: (t, 0)),
                  pl.BlockSpec((1, D), lambda t, h: (0, 0)),
                  pl.BlockSpec((D, PEER_QDIM), lambda t, h: (0, h)),
                  pl.BlockSpec((1, 2, N_KEYS, PEER_HALF), lambda t, h: (h, 0, 0, 0))],
        out_specs=(pl.BlockSpec((SEL_TOKENS, D), lambda t, h: (t, 0)), gtab_spec, gtab_spec, tab_spec, tab_spec),
        scratch_shapes=[pltpu.VMEM((SEL_TOKENS, D), MM_DTYPE)],
        compiler_params=pltpu.CompilerParams(dimension_semantics=("parallel", "arbitrary"),
                                             vmem_limit_bytes=VMEM_LIMIT),
        name="peer_select",
    )(x, norm2_g.reshape(1, D), w_q.astype(MM_DTYPE), sub_keys.astype(MM_DTYPE))

    eb = MIX_KEYS * N_KEYS
    vt_tiles = jnp.swapaxes(v.astype(MM_DTYPE).reshape(N_EXPERTS // eb, eb, D), 1, 2)
    mix_tab = pl.BlockSpec((PEER_HEADS * N_KEYS, MIX_TOKENS), lambda t, e: (0, t))
    mix_gtab = pl.BlockSpec((PEER_HEADS * N_KEYS // GATE_PACK, MIX_TOKENS), lambda t, e: (0, t))
    tok_blk = pl.BlockSpec((MIX_TOKENS, D), lambda t, e: (t, 0))
    return pl.pallas_call(
        _peer_mix_kernel,
        out_shape=jax.ShapeDtypeStruct((T, D), f32),
        grid=(T // MIX_TOKENS, N_KEYS // MIX_KEYS),
        in_specs=[tok_blk,
                  pl.BlockSpec((eb, D), lambda t, e: (e, 0)),
                  pl.BlockSpec((1, D, eb), lambda t, e: (e, 0, 0)),
                  mix_gtab, mix_gtab, mix_tab, mix_tab,
                  tok_blk,
                  pl.BlockSpec((1, D), lambda t, e: (0, 0))],
        out_specs=tok_blk,
        scratch_shapes=[pltpu.VMEM((D, MIX_TOKENS), f32),
                        pltpu.VMEM((eb, MIX_TOKENS), f32),
                        pltpu.VMEM((eb, MIX_TOKENS), MM_DTYPE)],
        compiler_params=pltpu.CompilerParams(dimension_semantics=("parallel", "arbitrary"),
                                             vmem_limit_bytes=VMEM_LIMIT),
        name="peer_mix",
    )(h, u.astype(MM_DTYPE), vt_tiles, r1, e1, n0, c0, x, final_g.reshape(1, D))


NA_BAND = NA_ROWS * GRID_W
NA_MASKED = -1e30


def _na_bias_table(rpb):
    cq = jnp.arange(GRID_W)[:, None]
    ck = jnp.arange(GRID_W)[None, :]
    cs = jnp.clip(cq - NA_COLS // 2, 0, GRID_W - NA_COLS)
    valid = (ck >= cs) & (ck < cs + NA_COLS)
    cb = jnp.clip(ck - cq + NA_COLS - 1, 0, 2 * NA_COLS - 2)
    rb = jnp.arange(NA_ROWS)[:, None] + jnp.arange(NA_ROWS)[None, :]
    tab = rpb[:, rb][:, :, :, cb]
    tab = jnp.where(valid, tab, NA_MASKED)
    return jnp.transpose(tab, (1, 0, 3, 2, 4)).reshape(NA_ROWS, NA_HEADS, GRID_W, NA_BAND)


def _na_row_start(r, rows):
    return jnp.clip(r - NA_ROWS // 2, 0, rows - NA_ROWS)


def _na_kernel(q_ref, k_ref, v_ref, tab_ref, o_ref, *, rows):
    start = pl.multiple_of(_na_row_start(pl.program_id(1), rows) * GRID_W, GRID_W)
    kb = k_ref[0, pl.ds(start, NA_BAND), :]
    vb = v_ref[0, pl.ds(start, NA_BAND), :]
    q = q_ref[0]
    lane = lax.broadcasted_iota(jnp.int32, (GRID_W, LANES), 1)
    lo = lane < NA_HEAD_DIM
    pairs = [slice(hp * LANES, (hp + 1) * LANES) for hp in range(NA_HEADS // 2)]
    scores = []
    for h in range(NA_HEADS):
        qp = q[:, pairs[h // 2]]
        mine = lo if h % 2 == 0 else jnp.logical_not(lo)
        scores.append(_dot_nt(jnp.where(mine, qp, jnp.zeros_like(qp)), kb[:, pairs[h // 2]]) + tab_ref[0, h])
    probs, sums = [], []
    for sc in scores:
        p = jnp.exp(sc - jnp.max(sc, axis=-1, keepdims=True))
        sums.append(jnp.sum(p, axis=-1, keepdims=True))
        probs.append(p.astype(vb.dtype))
    outs = []
    for hp in range(NA_HEADS // 2):
        o = [jnp.dot(probs[2 * hp + s], vb[:, pairs[hp]], preferred_element_type=jnp.float32) / sums[2 * hp + s]
             for s in range(2)]
        outs.append(jnp.where(lo, o[0], o[1]))
    o_ref[0] = jnp.concatenate(outs, axis=-1).astype(o_ref.dtype)


def neighbourhood_attention_pallas(q, k, v, rpb):
    B, L, W = q.shape
    rows = L // GRID_W
    assert rows >= NA_ROWS and W == NA_HEADS * NA_HEAD_DIM and 2 * NA_HEAD_DIM == LANES
    q = (q * NA_HEAD_DIM ** -0.5).astype(MM_DTYPE)
    seq = pl.BlockSpec((1, L, W), lambda b, r: (b, 0, 0))
    row = pl.BlockSpec((1, GRID_W, W), lambda b, r: (b, r, 0))
    tab = pl.BlockSpec((1, NA_HEADS, GRID_W, NA_BAND),
                       lambda b, r: (_na_row_start(r, rows) - r + NA_ROWS - 1, 0, 0, 0))
    return pl.pallas_call(
        functools.partial(_na_kernel, rows=rows),
        out_shape=jax.ShapeDtypeStruct((B, L, W), jnp.float32),
        grid=(B, rows),
        in_specs=[row, seq, seq, tab],
        out_specs=row,
        compiler_params=pltpu.CompilerParams(dimension_semantics=("parallel", "arbitrary"),
                                             vmem_limit_bytes=VMEM_LIMIT),
        name="na_attention",
    )(q, k.astype(MM_DTYPE), v.astype(MM_DTYPE), _na_bias_table(rpb))


SUBLANES = 8


def _rotate_half(x, cos, sin_signed):
    return x * cos + pltpu.roll(x, x.shape[-1] // 2, axis=1) * sin_signed


def _dot_tn(a, b):
    return lax.dot_general(a, b, (((0,), (0,)), ((), ())), preferred_element_type=jnp.float32)


def _ret_kernel(q_ref, k_ref, v_ref, g_ref, cos_ref, sin_ref, df_ref, db_ref, ng_ref, o_ref,
                of_ref, ob_ref, *, nc):
    C = RET_CHUNK
    mm = o_ref.dtype
    lg_f = jax.nn.log_sigmoid(df_ref[0])[0:1, :]
    lg_b = jax.nn.log_sigmoid(db_ref[0])[0:1, :]
    pos = lax.broadcasted_iota(jnp.int32, (C, C), 0).astype(jnp.float32)
    diff = pos - lax.broadcasted_iota(jnp.int32, (C, C), 1).astype(jnp.float32)
    dec_f = jnp.where(diff >= 0, jnp.exp(lg_f * jnp.maximum(diff, 0.0)), 0.0)
    dec_b = jnp.where(diff < 0, jnp.exp(lg_b * jnp.maximum(-diff, 0.0)), 0.0)
    kdec_f = jnp.exp(lg_f * (C - 1 - pos))
    qdec_f = jnp.exp(lg_f * (pos + 1))
    kdec_b = jnp.exp(lg_b * pos)
    qdec_b = jnp.exp(lg_b * (C - pos))
    step_f = jnp.exp(lg_f * C)
    step_b = jnp.exp(lg_b * C)
    scale = RET_HEAD_DIM ** -0.5

    def load(c):
        sl = pl.ds(pl.multiple_of(c * C, C), C)
        q = _rotate_half(q_ref[0, sl, :], cos_ref[sl, :], sin_ref[sl, :])
        k = _rotate_half(k_ref[0, sl, :], cos_ref[sl, :], sin_ref[sl, :]) * scale
        return sl, q, k, v_ref[0, sl, :].astype(mm)

    def body(c, carry):
        rf, rb = carry
        sl_f, q_f, k_f, v_f = load(c)
        sl_b, q_b, k_b, v_b = load(nc - 1 - c)
        att_f = _dot_nt(q_f.astype(mm), k_f.astype(mm))
        att_b = _dot_nt(q_b.astype(mm), k_b.astype(mm))
        inter_f = jnp.dot((q_f * qdec_f).astype(mm), rf.astype(mm), preferred_element_type=jnp.float32)
        inter_b = jnp.dot((q_b * qdec_b).astype(mm), rb.astype(mm), preferred_element_type=jnp.float32)
        s_f = _dot_tn((k_f * kdec_f).astype(mm), v_f)
        s_b = _dot_tn((k_b * kdec_b).astype(mm), v_b)
        of_ref[sl_f, :] = inter_f + jnp.dot((att_f * dec_f).astype(mm), v_f, preferred_element_type=jnp.float32)
        ob_ref[sl_b, :] = inter_b + jnp.dot((att_b * dec_b).astype(mm), v_b, preferred_element_type=jnp.float32)
        return step_f * rf + s_f, step_b * rb + s_b

    zero = jnp.zeros((RET_HEAD_DIM, RET_HEAD_DIM), jnp.float32)
    lax.fori_loop(0, nc, body, (zero, zero))

    def finish(c, carry):
        sl = pl.ds(pl.multiple_of(c * C, C), C)
        o = of_ref[sl, :] + ob_ref[sl, :]
        mu = jnp.mean(o, axis=-1, keepdims=True)
        var = jnp.mean(jnp.square(o - mu), axis=-1, keepdims=True)
        o = (o - mu) * lax.rsqrt(var + EPS) * ng_ref[...]
        o_ref[0, sl, :] = (jax.nn.silu(g_ref[0, sl, :]) * o).astype(mm)
        return carry

    lax.fori_loop(0, nc, finish, 0)


def retention_pallas(q, k, v, gate, decay_fwd, decay_bwd, norm_g):
    B, L, W = q.shape
    dh = RET_HEAD_DIM
    assert W == RET_HEADS * dh and dh == LANES and L % RET_CHUNK == 0
    inv = 1.0 / (ROPE_BASE ** (jnp.arange(0, dh, 2, dtype=jnp.float32) / dh))
    ang = jnp.arange(L, dtype=jnp.float32)[:, None] * inv[None, :]
    cos = jnp.concatenate([jnp.cos(ang), jnp.cos(ang)], axis=-1)
    sin = jnp.concatenate([-jnp.sin(ang), jnp.sin(ang)], axis=-1)
    bcast = lambda d: jnp.broadcast_to(d.astype(jnp.float32)[:, None, None], (RET_HEADS, SUBLANES, dh))
    seq = pl.BlockSpec((1, L, dh), lambda b, h: (b, 0, h))
    tab = pl.BlockSpec((L, dh), lambda b, h: (0, 0))
    dec = pl.BlockSpec((1, SUBLANES, dh), lambda b, h: (h, 0, 0))
    return pl.pallas_call(
        functools.partial(_ret_kernel, nc=L // RET_CHUNK),
        out_shape=jax.ShapeDtypeStruct((B, L, W), MM_DTYPE),
        grid=(B, RET_HEADS),
        in_specs=[seq, seq, seq, seq, tab, tab, dec, dec, pl.BlockSpec((1, dh), lambda b, h: (0, h))],
        out_specs=seq,
        scratch_shapes=[pltpu.VMEM((L, dh), jnp.float32), pltpu.VMEM((L, dh), jnp.float32)],
        compiler_params=pltpu.CompilerParams(dimension_semantics=("parallel", "parallel"),
                                             vmem_limit_bytes=VMEM_LIMIT),
        name="retention",
    )(q, k, v, gate, cos, sin, bcast(decay_fwd), bcast(decay_bwd), norm_g.reshape(1, W))


PROJ_TOKENS = 512
N_NA_STREAMS = 3
N_RET_STREAMS = 4


def _in_proj_kernel(x_ref, g_ref, w_ref, *out_refs):
    x = x_ref[...]
    h = (x * lax.rsqrt(jnp.mean(x * x, axis=-1, keepdims=True) + EPS) * g_ref[...]).astype(w_ref.dtype)
    for j, o_ref in enumerate(out_refs):
        o_ref[...] = jnp.dot(h, w_ref[:, j * NA_WIDTH:(j + 1) * NA_WIDTH],
                             preferred_element_type=jnp.float32).astype(o_ref.dtype)


def in_projection(x, norm_g, w_in):
    T, D = x.shape
    assert NA_WIDTH == RET_WIDTH and T % PROJ_TOKENS == 0
    n = N_NA_STREAMS + N_RET_STREAMS
    assert w_in.shape == (D, n * NA_WIDTH)
    tok = pl.BlockSpec((PROJ_TOKENS, D), lambda t: (t, 0))
    out = pl.BlockSpec((PROJ_TOKENS, NA_WIDTH), lambda t: (t, 0))
    return pl.pallas_call(
        _in_proj_kernel,
        out_shape=tuple(jax.ShapeDtypeStruct((T, NA_WIDTH), jnp.float32) for _ in range(n)),
        grid=(T // PROJ_TOKENS,),
        in_specs=[tok, pl.BlockSpec((1, D), lambda t: (0, 0)), pl.BlockSpec(w_in.shape, lambda t: (0, 0))],
        out_specs=tuple(out for _ in range(n)),
        compiler_params=pltpu.CompilerParams(dimension_semantics=("parallel",), vmem_limit_bytes=VMEM_LIMIT),
        name="in_projection",
    )(x, norm_g.reshape(1, D), w_in.astype(MM_DTYPE))


def _out_proj_kernel(a_ref, r_ref, w_ref, x_ref, o_ref):
    o = jnp.dot(a_ref[...].astype(w_ref.dtype), w_ref[:NA_WIDTH, :], preferred_element_type=jnp.float32)
    o = o + jnp.dot(r_ref[...].astype(w_ref.dtype), w_ref[NA_WIDTH:, :], preferred_element_type=jnp.float32)
    o_ref[...] = x_ref[...] + o


def out_projection(na_o, ret_o, w_out, x):
    T, D = x.shape
    tok = pl.BlockSpec((PROJ_TOKENS, D), lambda t: (t, 0))
    half = pl.BlockSpec((PROJ_TOKENS, NA_WIDTH), lambda t: (t, 0))
    return pl.pallas_call(
        _out_proj_kernel,
        out_shape=jax.ShapeDtypeStruct((T, D), jnp.float32),
        grid=(T // PROJ_TOKENS,),
        in_specs=[half, half, pl.BlockSpec(w_out.shape, lambda t: (0, 0)), tok],
        out_specs=tok,
        compiler_params=pltpu.CompilerParams(dimension_semantics=("parallel",), vmem_limit_bytes=VMEM_LIMIT),
        name="out_projection",
    )(na_o, ret_o, w_out.astype(MM_DTYPE), x)


def encoder_layer(x, norm1_g, w_in, na_rpb, ret_decay_fwd, ret_decay_bwd, ret_norm_g,
                  w_out, norm2_g, peer_w_q, peer_sub_keys, peer_u, peer_v, final_g):
    B, L, D = x.shape
    xt = x.reshape(B * L, D)
    streams = [s.reshape(B, L, NA_WIDTH) for s in in_projection(xt, norm1_g, w_in)]
    na_o = neighbourhood_attention_pallas(*streams[:N_NA_STREAMS], na_rpb)
    ret_o = retention_pallas(*streams[N_NA_STREAMS:], ret_decay_fwd, ret_decay_bwd, ret_norm_g)
    x1 = out_projection(na_o.reshape(B * L, NA_WIDTH), ret_o.reshape(B * L, RET_WIDTH), w_out, xt)
    y = peer_and_final_norm(x1, norm2_g, peer_w_q, peer_sub_keys, peer_u, peer_v, final_g)
    return y.reshape(B, L, D)


def kernel(x_prompt, x_sample, norm1_g, w_in, na_rpb, ret_decay_fwd, ret_decay_bwd, ret_norm_g,
           w_out, norm2_g, peer_w_q, peer_sub_keys, peer_u, peer_v, final_g):
    assert DEPTH == 1

    def trunk(x):
        return encoder_layer(x, norm1_g[0], w_in[0], na_rpb[0], ret_decay_fwd[0], ret_decay_bwd[0],
                             ret_norm_g[0], w_out[0], norm2_g[0], peer_w_q[0], peer_sub_keys[0],
                             peer_u[0], peer_v[0], final_g)

    return (trunk(x_prompt), trunk(x_sample))
```

```python
import functools
import math
import jax, jax.numpy as jnp
from jax import lax
from jax.experimental import pallas as pl
from jax.experimental.pallas import tpu as pltpu

D_MODEL = 1024
DEPTH = 1
GRID_W = 64
NA_HEADS = 8
NA_HEAD_DIM = 64
NA_WIDTH = NA_HEADS * NA_HEAD_DIM
NA_ROWS = 8
NA_COLS = 16
RET_HEADS = 4
RET_HEAD_DIM = 128
RET_WIDTH = RET_HEADS * RET_HEAD_DIM
RET_CHUNK = 128
ROPE_BASE = 10000.0
PEER_HEADS = 8
PEER_QDIM = 256
PEER_HALF = PEER_QDIM // 2
N_KEYS = 128
N_EXPERTS = N_KEYS * N_KEYS
PEER_TOPK = 16
EPS = 1e-6

LANES = 128
SUBLANES = 8
VMEM_LIMIT = 56 * 1024 * 1024
MM_DTYPE = jnp.bfloat16


def _dot_nt(a, b):
    return lax.dot_general(a, b, (((1,), (1,)), ((), ())), preferred_element_type=jnp.float32)


def _dot_tn(a, b):
    return lax.dot_general(a, b, (((0,), (0,)), ((), ())), preferred_element_type=jnp.float32)


GATE_DTYPE = jnp.bfloat16
GATE_PACK = 2
NOT_TOP = 255.0
SEL_TOKENS = 512
MIX_TOKENS = 512
MIX_KEYS = 16


def _topk_rank(s):
    iota = lax.broadcasted_iota(jnp.int32, s.shape, 0).astype(jnp.float32)
    rank = jnp.full(s.shape, NOT_TOP, jnp.float32)
    tops = []
    for r in range(PEER_TOPK):
        m = jnp.max(s, axis=0, keepdims=True)
        idx = jnp.min(jnp.where(s == m, iota, float(s.shape[0])), axis=0, keepdims=True)
        hit = iota == idx
        rank = jnp.where(hit, float(r), rank)
        s = jnp.where(hit, -jnp.inf, s)
        tops.append(m)
    return rank, tops


def _topk_rank_untied(s):
    rank = jnp.full(s.shape, NOT_TOP, jnp.float32)
    tops = []
    for r in range(PEER_TOPK):
        m = jnp.max(s, axis=0, keepdims=True)
        hit = s == m
        rank = jnp.where(hit, float(r), rank)
        s = jnp.where(hit, -jnp.inf, s)
        tops.append(m)
    removed = jnp.sum(jnp.where(rank < NOT_TOP, 1.0, 0.0), axis=0, keepdims=True)
    return rank, tops, removed


def _young_counts(a_top, b_top):
    k = PEER_TOPK
    w = a_top[0].shape[1]
    b16 = jnp.concatenate(b_top, axis=0)
    b8 = jnp.concatenate(b_top[:8], axis=0)
    blocks = [a_top[0] + b16] + [a_top[a] + b8 for a in range(1, 8)]
    blocks.append(jnp.concatenate(a_top[8:], axis=0) + b_top[0])
    cand = jnp.concatenate(blocks, axis=0)
    rows = cand.shape[0]
    rho = lax.broadcasted_iota(jnp.int32, (rows, w), 0)
    mid = rho - 16
    a_id = jnp.where(rho < 16, 0, jnp.where(rho < 72, 1 + (mid >> 3), rho - 64))
    b_id = jnp.where(rho < 16, rho, jnp.where(rho < 72, mid & 7, 0))
    pos = (a_id * k + b_id).astype(jnp.float32)
    cand = jnp.where((a_id + 1) * (b_id + 1) <= k, cand, -jnp.inf)
    a_iota = lax.broadcasted_iota(jnp.int32, (k, w), 0).astype(jnp.float32)
    n = jnp.zeros((k, w), jnp.float32)
    best = []
    for _ in range(k):
        m = jnp.max(cand, axis=0, keepdims=True)
        p = jnp.min(jnp.where(cand == m, pos, float(k * k)), axis=0, keepdims=True)
        cand = jnp.where(pos == p, -jnp.inf, cand)
        n = n + (a_iota == jnp.floor(p * (1.0 / k))).astype(jnp.float32)
        best.append(m)
    return n, best


def _peer_select_kernel(x_ref, g_ref, wq_ref, keys_ref, h_ref, r1_ref, e1_ref, n0_ref, c0_ref, hb_ref):
    hd = pl.program_id(1)

    @pl.when(hd == 0)
    def _():
        x = x_ref[...]
        y = x * lax.rsqrt(jnp.mean(x * x, axis=-1, keepdims=True) + EPS) * g_ref[...]
        hb_ref[...] = y.astype(hb_ref.dtype)
        h_ref[...] = y.astype(h_ref.dtype)

    def emit(c, s0, s1, rank0, a_top, rank1, b_top):
        n, best = _young_counts(a_top, b_top)
        z = jnp.zeros_like(best[0])
        for m in best:
            z = z + jnp.exp(m - best[0])
        n0 = jnp.zeros_like(s0)
        for a in range(PEER_TOPK):
            n0 = jnp.where(rank0 == float(a), n[a:a + 1, :], n0)
        sl = slice(c * LANES, (c + 1) * LANES)
        e1 = jnp.where(rank1 < NOT_TOP, jnp.exp(s1 - b_top[0]), 0.0)
        r1_ref[:, sl] = pltpu.bitcast(rank1.astype(GATE_DTYPE), r1_ref.dtype)
        e1_ref[:, sl] = pltpu.bitcast(e1.astype(GATE_DTYPE), e1_ref.dtype)
        n0_ref[:, sl] = n0
        c0_ref[:, sl] = jnp.exp(s0 - a_top[0]) * (0.5 / z)

    chunks = range(SEL_TOKENS // LANES)
    scores = []
    qs = [jnp.dot(hb_ref[c * LANES:(c + 1) * LANES, :], wq_ref[...],
                  preferred_element_type=jnp.float32).astype(keys_ref.dtype) for c in chunks]
    for qc in qs:
        scores.append((_dot_nt(keys_ref[0, 0], qc[:, :PEER_HALF]),
                       _dot_nt(keys_ref[0, 1], qc[:, PEER_HALF:])))
    untied = [(_topk_rank_untied(s0), _topk_rank_untied(s1)) for s0, s1 in scores]
    removed = functools.reduce(jnp.maximum, [t[2] for pair in untied for t in pair])
    tied = jnp.max(removed) > float(PEER_TOPK)

    @pl.when(tied)
    def _():
        for c, (s0, s1) in zip(chunks, scores):
            emit(c, s0, s1, *_topk_rank(s0), *_topk_rank(s1))

    @pl.when(jnp.logical_not(tied))
    def _():
        for c, (s0, s1), (t0, t1) in zip(chunks, scores, untied):
            emit(c, s0, s1, t0[0], t0[1], t1[0], t1[1])


def _twice_gelu_tanh(x):
    c = math.sqrt(2.0 / math.pi)
    return x * (1.0 + jnp.tanh(x * (c + (c * 0.044715) * (x * x))))


def _peer_mix_kernel(h_ref, u_ref, vt_ref, r1_ref, e1_ref, n0_ref, c0_ref, x_ref, fg_ref, y_ref,
                     acc_ref, a_ref, w_ref):
    e = pl.program_id(1)

    @pl.when(e == 0)
    def _():
        acc_ref[...] = jnp.zeros_like(acc_ref)

    gd = GATE_DTYPE
    pack = SUBLANES * GATE_PACK
    krows = N_KEYS // GATE_PACK
    half = MIX_TOKENS // 2
    for hf in range(2):
        a_ref[:, hf * half:(hf + 1) * half] = _dot_nt(u_ref[...], h_ref[hf * half:(hf + 1) * half, :])
    rows = [pl.ds(pl.multiple_of(hd * N_KEYS + e * MIX_KEYS, MIX_KEYS), MIX_KEYS) for hd in range(PEER_HEADS)]
    for hf in range(2):
        for c in range(hf * half // LANES, (hf + 1) * half // LANES):
            sl = slice(c * LANES, (c + 1) * LANES)
            n_t = [n0_ref[r, sl] for r in rows]
            c_t = [c0_ref[r, sl] for r in rows]
            for ii in range(MIX_KEYS):
                g = jnp.zeros((N_KEYS // pack, pack, LANES), gd)
                for hd in range(PEER_HEADS):
                    n_b = jnp.broadcast_to(n_t[hd][ii:ii + 1, :], (pack, LANES)).astype(gd)[None]
                    c_b = jnp.broadcast_to(c_t[hd][ii:ii + 1, :], (pack, LANES)).astype(gd)[None]
                    r1 = pltpu.bitcast(r1_ref[hd * krows:(hd + 1) * krows, sl], gd).reshape(g.shape)
                    e1 = pltpu.bitcast(e1_ref[hd * krows:(hd + 1) * krows, sl], gd).reshape(g.shape)
                    g = g + jnp.where(r1 < n_b, e1, jnp.zeros_like(e1)) * c_b
                act = _twice_gelu_tanh(a_ref[ii * N_KEYS:(ii + 1) * N_KEYS, sl])
                gate = g.reshape(N_KEYS, LANES).astype(jnp.float32)
                w_ref[ii * N_KEYS:(ii + 1) * N_KEYS, sl] = (act * gate).astype(w_ref.dtype)
        hs = slice(hf * half, (hf + 1) * half)
        acc_ref[:, hs] += jnp.dot(vt_ref[...], w_ref[:, hs], preferred_element_type=jnp.float32)

    @pl.when(e == pl.num_programs(1) - 1)
    def _():
        o = acc_ref[...].T + x_ref[...]
        y_ref[...] = o * lax.rsqrt(jnp.mean(o * o, axis=-1, keepdims=True) + EPS) * fg_ref[...]


def peer_and_final_norm(x, norm2_g, w_q, sub_keys, u, v, final_g):
    T, D = x.shape
    assert T % MIX_TOKENS == 0 and T % SEL_TOKENS == 0 and N_KEYS % MIX_KEYS == 0 and MIX_KEYS % SUBLANES == 0
    f32 = jnp.float32
    tab = jax.ShapeDtypeStruct((PEER_HEADS * N_KEYS, T), f32)
    gtab = jax.ShapeDtypeStruct((PEER_HEADS * N_KEYS // GATE_PACK, T), jnp.uint32)
    tab_spec = pl.BlockSpec((N_KEYS, SEL_TOKENS), lambda t, h: (h, t))
    gtab_spec = pl.BlockSpec((N_KEYS // GATE_PACK, SEL_TOKENS), lambda t, h: (h, t))
    h, r1, e1, n0, c0 = pl.pallas_call(
        _peer_select_kernel,
        out_shape=(jax.ShapeDtypeStruct((T, D), MM_DTYPE), gtab, gtab, tab, tab),
        grid=(T // SEL_TOKENS, PEER_HEADS),
        in_specs=[pl.BlockSpec((SEL_TOKENS, D), lambda t, h: (t, 0)),
                  pl.BlockSpec((1, D), lambda t, h: (0, 0)),
                  pl.BlockSpec((D, PEER_QDIM), lambda t, h: (0, h)),
                  pl.BlockSpec((1, 2, N_KEYS, PEER_HALF), lambda t, h: (h, 0, 0, 0))],
        out_specs=(pl.BlockSpec((SEL_TOKENS, D), lambda t, h: (t, 0)), gtab_spec, gtab_spec, tab_spec, tab_spec),
        scratch_shapes=[pltpu.VMEM((SEL_TOKENS, D), MM_DTYPE)],
        compiler_params=pltpu.CompilerParams(dimension_semantics=("parallel", "arbitrary"),
                                             vmem_limit_bytes=VMEM_LIMIT),
        name="peer_select",
    )(x, norm2_g.reshape(1, D), w_q.astype(MM_DTYPE), sub_keys.astype(MM_DTYPE))

    eb = MIX_KEYS * N_KEYS
    mix_tab = pl.BlockSpec((PEER_HEADS * N_KEYS, MIX_TOKENS), lambda t, e: (0, t))
    mix_gtab = pl.BlockSpec((PEER_HEADS * N_KEYS // GATE_PACK, MIX_TOKENS), lambda t, e: (0, t))
    tok_blk = pl.BlockSpec((MIX_TOKENS, D), lambda t, e: (t, 0))
    return pl.pallas_call(
        _peer_mix_kernel,
        out_shape=jax.ShapeDtypeStruct((T, D), f32),
        grid=(T // MIX_TOKENS, N_KEYS // MIX_KEYS),
        in_specs=[tok_blk,
                  pl.BlockSpec((eb, D), lambda t, e: (e, 0)),
                  pl.BlockSpec((D, eb), lambda t, e: (0, e)),
                  mix_gtab, mix_gtab, mix_tab, mix_tab,
                  tok_blk,
                  pl.BlockSpec((1, D), lambda t, e: (0, 0))],
        out_specs=tok_blk,
        scratch_shapes=[pltpu.VMEM((D, MIX_TOKENS), f32),
                        pltpu.VMEM((eb, MIX_TOKENS), f32),
                        pltpu.VMEM((eb, MIX_TOKENS), MM_DTYPE)],
        compiler_params=pltpu.CompilerParams(dimension_semantics=("parallel", "arbitrary"),
                                             vmem_limit_bytes=VMEM_LIMIT),
        name="peer_mix",
    )(h, u.astype(MM_DTYPE), v.T.astype(MM_DTYPE), r1, e1, n0, c0, x, final_g.reshape(1, D))


NA_BAND = NA_ROWS * GRID_W
NA_MASKED = -1e30
NA_STEP_ROWS = 2


def _na_bias_table(rpb):
    cq = jnp.arange(GRID_W)[:, None]
    ck = jnp.arange(GRID_W)[None, :]
    cs = jnp.clip(cq - NA_COLS // 2, 0, GRID_W - NA_COLS)
    valid = (ck >= cs) & (ck < cs + NA_COLS)
    cb = jnp.clip(ck - cq + NA_COLS - 1, 0, 2 * NA_COLS - 2)
    rb = jnp.arange(NA_ROWS)[:, None] + jnp.arange(NA_ROWS)[None, :]
    tab = rpb[:, rb][:, :, :, cb]
    tab = jnp.where(valid, tab, NA_MASKED)
    return jnp.transpose(tab, (1, 0, 3, 2, 4)).reshape(NA_ROWS, NA_HEADS, GRID_W, NA_BAND)


def _na_row_start(r, rows):
    return jnp.clip(r - NA_ROWS // 2, 0, rows - NA_ROWS)


def _na_kernel(q_ref, k_ref, v_ref, *rest, rows):
    tab_refs, o_ref = rest[:NA_STEP_ROWS], rest[NA_STEP_ROWS]
    lane = lax.broadcasted_iota(jnp.int32, (GRID_W, LANES), 1)
    lo = lane < NA_HEAD_DIM
    pairs = [slice(hp * LANES, (hp + 1) * LANES) for hp in range(NA_HEADS // 2)]
    bands, scores = [], []
    for j in range(NA_STEP_ROWS):
        r = pl.program_id(1) * NA_STEP_ROWS + j
        start = pl.multiple_of(_na_row_start(r, rows) * GRID_W, GRID_W)
        kb = k_ref[0, pl.ds(start, NA_BAND), :]
        bands.append(v_ref[0, pl.ds(start, NA_BAND), :])
        q = q_ref[0, j * GRID_W:(j + 1) * GRID_W, :]
        for h in range(NA_HEADS):
            qp = q[:, pairs[h // 2]]
            mine = lo if h % 2 == 0 else jnp.logical_not(lo)
            scores.append(_dot_nt(jnp.where(mine, qp, jnp.zeros_like(qp)), kb[:, pairs[h // 2]])
                          + tab_refs[j][0, h])
    probs, sums = [], []
    for sc in scores:
        p = jnp.exp(sc - jnp.max(sc, axis=-1, keepdims=True))
        sums.append(jnp.sum(p, axis=-1, keepdims=True))
        probs.append(p.astype(bands[0].dtype))
    for j in range(NA_STEP_ROWS):
        outs = []
        for hp in range(NA_HEADS // 2):
            i0 = j * NA_HEADS + 2 * hp
            o = [jnp.dot(probs[i0 + s], bands[j][:, pairs[hp]], preferred_element_type=jnp.float32) / sums[i0 + s]
                 for s in range(2)]
            outs.append(jnp.where(lo, o[0], o[1]))
        o_ref[0, j * GRID_W:(j + 1) * GRID_W, :] = jnp.concatenate(outs, axis=-1).astype(o_ref.dtype)


def neighbourhood_attention_pallas(q, k, v, rpb):
    B, L, W = q.shape
    rows = L // GRID_W
    assert rows >= NA_ROWS and rows % NA_STEP_ROWS == 0 and W == NA_HEADS * NA_HEAD_DIM and 2 * NA_HEAD_DIM == LANES
    seq = pl.BlockSpec((1, L, W), lambda b, r: (b, 0, 0))
    row = pl.BlockSpec((1, NA_STEP_ROWS * GRID_W, W), lambda b, r: (b, r, 0))

    def tab(j):
        def index(b, r):
            rr = r * NA_STEP_ROWS + j
            return (_na_row_start(rr, rows) - rr + NA_ROWS - 1, 0, 0, 0)
        return pl.BlockSpec((1, NA_HEADS, GRID_W, NA_BAND), index)

    bias = _na_bias_table(rpb)
    return pl.pallas_call(
        functools.partial(_na_kernel, rows=rows),
        out_shape=jax.ShapeDtypeStruct((B, L, W), MM_DTYPE),
        grid=(B, rows // NA_STEP_ROWS),
        in_specs=[row, seq, seq] + [tab(j) for j in range(NA_STEP_ROWS)],
        out_specs=row,
        compiler_params=pltpu.CompilerParams(dimension_semantics=("parallel", "arbitrary"),
                                             vmem_limit_bytes=VMEM_LIMIT),
        name="na_attention",
    )(q, k, v, *([bias] * NA_STEP_ROWS))


def _rotate_half(x, cos, sin_signed):
    return x * cos + pltpu.roll(x, x.shape[-1] // 2, axis=1) * sin_signed


def _ret_kernel(q_ref, k_ref, v_ref, g_ref, cos_ref, sin_ref, df_ref, db_ref, ng_ref, o_ref,
                of_ref, ob_ref, *, nc):
    C = RET_CHUNK
    mm = o_ref.dtype
    lg_f = jax.nn.log_sigmoid(df_ref[0])[0:1, :]
    lg_b = jax.nn.log_sigmoid(db_ref[0])[0:1, :]
    pos = lax.broadcasted_iota(jnp.int32, (C, C), 0).astype(jnp.float32)
    diff = pos - lax.broadcasted_iota(jnp.int32, (C, C), 1).astype(jnp.float32)
    dec_f = jnp.where(diff >= 0, jnp.exp(lg_f * jnp.maximum(diff, 0.0)), 0.0)
    dec_b = jnp.where(diff < 0, jnp.exp(lg_b * jnp.maximum(-diff, 0.0)), 0.0)
    kdec_f = jnp.exp(lg_f * (C - 1 - pos))
    qdec_f = jnp.exp(lg_f * (pos + 1))
    kdec_b = jnp.exp(lg_b * pos)
    qdec_b = jnp.exp(lg_b * (C - pos))
    step_f = jnp.exp(lg_f * C)
    step_b = jnp.exp(lg_b * C)
    scale = RET_HEAD_DIM ** -0.5

    def load(c):
        sl = pl.ds(pl.multiple_of(c * C, C), C)
        q = _rotate_half(q_ref[0, sl, :], cos_ref[sl, :], sin_ref[sl, :])
        k = _rotate_half(k_ref[0, sl, :], cos_ref[sl, :], sin_ref[sl, :]) * scale
        return sl, q, k, v_ref[0, sl, :].astype(mm)

    def body(c, carry):
        rf, rb = carry
        sl_f, q_f, k_f, v_f = load(c)
        sl_b, q_b, k_b, v_b = load(nc - 1 - c)
        att_f = _dot_nt(q_f.astype(mm), k_f.astype(mm))
        att_b = _dot_nt(q_b.astype(mm), k_b.astype(mm))
        inter_f = jnp.dot((q_f * qdec_f).astype(mm), rf.astype(mm), preferred_element_type=jnp.float32)
        inter_b = jnp.dot((q_b * qdec_b).astype(mm), rb.astype(mm), preferred_element_type=jnp.float32)
        s_f = _dot_tn((k_f * kdec_f).astype(mm), v_f)
        s_b = _dot_tn((k_b * kdec_b).astype(mm), v_b)
        of_ref[sl_f, :] = inter_f + jnp.dot((att_f * dec_f).astype(mm), v_f, preferred_element_type=jnp.float32)
        ob_ref[sl_b, :] = inter_b + jnp.dot((att_b * dec_b).astype(mm), v_b, preferred_element_type=jnp.float32)
        return step_f * rf + s_f, step_b * rb + s_b

    zero = jnp.zeros((RET_HEAD_DIM, RET_HEAD_DIM), jnp.float32)
    lax.fori_loop(0, nc, body, (zero, zero))

    def finish(c, carry):
        sl = pl.ds(pl.multiple_of(c * C, C), C)
        o = of_ref[sl, :] + ob_ref[sl, :]
        mu = jnp.mean(o, axis=-1, keepdims=True)
        var = jnp.mean(jnp.square(o - mu), axis=-1, keepdims=True)
        o = (o - mu) * lax.rsqrt(var + EPS) * ng_ref[...]
        o_ref[0, sl, :] = (jax.nn.silu(g_ref[0, sl, :]) * o).astype(mm)
        return carry

    lax.fori_loop(0, nc, finish, 0)


def retention_pallas(q, k, v, gate, decay_fwd, decay_bwd, norm_g):
    B, L, W = q.shape
    dh = RET_HEAD_DIM
    assert W == RET_HEADS * dh and dh == LANES and L % RET_CHUNK == 0
    inv = 1.0 / (ROPE_BASE ** (jnp.arange(0, dh, 2, dtype=jnp.float32) / dh))
    ang = jnp.arange(L, dtype=jnp.float32)[:, None] * inv[None, :]
    cos = jnp.concatenate([jnp.cos(ang), jnp.cos(ang)], axis=-1)
    sin = jnp.concatenate([-jnp.sin(ang), jnp.sin(ang)], axis=-1)
    bcast = lambda d: jnp.broadcast_to(d.astype(jnp.float32)[:, None, None], (RET_HEADS, SUBLANES, dh))
    seq = pl.BlockSpec((1, L, dh), lambda b, h: (b, 0, h))
    tab = pl.BlockSpec((L, dh), lambda b, h: (0, 0))
    dec = pl.BlockSpec((1, SUBLANES, dh), lambda b, h: (h, 0, 0))
    return pl.pallas_call(
        functools.partial(_ret_kernel, nc=L // RET_CHUNK),
        out_shape=jax.ShapeDtypeStruct((B, L, W), MM_DTYPE),
        grid=(B, RET_HEADS),
        in_specs=[seq, seq, seq, seq, tab, tab, dec, dec, pl.BlockSpec((1, dh), lambda b, h: (0, h))],
        out_specs=seq,
        scratch_shapes=[pltpu.VMEM((L, dh), jnp.float32), pltpu.VMEM((L, dh), jnp.float32)],
        compiler_params=pltpu.CompilerParams(dimension_semantics=("parallel", "parallel"),
                                             vmem_limit_bytes=VMEM_LIMIT),
        name="retention",
    )(q, k, v, gate, cos, sin, bcast(decay_fwd), bcast(decay_bwd), norm_g.reshape(1, W))


PROJ_TOKENS = 512
N_NA_STREAMS = 3
N_RET_STREAMS = 4


def _in_proj_kernel(x_ref, g_ref, w_ref, *out_refs):
    x = x_ref[...]
    h = (x * lax.rsqrt(jnp.mean(x * x, axis=-1, keepdims=True) + EPS) * g_ref[...]).astype(w_ref.dtype)
    for j, o_ref in enumerate(out_refs):
        o = jnp.dot(h, w_ref[:, j * NA_WIDTH:(j + 1) * NA_WIDTH], preferred_element_type=jnp.float32)
        if j == 0:
            o = o * NA_HEAD_DIM ** -0.5
        o_ref[...] = o.astype(o_ref.dtype)


def in_projection(x, norm_g, w_in):
    T, D = x.shape
    assert NA_WIDTH == RET_WIDTH and T % PROJ_TOKENS == 0
    n = N_NA_STREAMS + N_RET_STREAMS
    assert w_in.shape == (D, n * NA_WIDTH)
    tok = pl.BlockSpec((PROJ_TOKENS, D), lambda t: (t, 0))
    out = pl.BlockSpec((PROJ_TOKENS, NA_WIDTH), lambda t: (t, 0))
    return pl.pallas_call(
        _in_proj_kernel,
        out_shape=tuple(jax.ShapeDtypeStruct((T, NA_WIDTH), MM_DTYPE if j < N_NA_STREAMS else jnp.float32)
                        for j in range(n)),
        grid=(T // PROJ_TOKENS,),
        in_specs=[tok, pl.BlockSpec((1, D), lambda t: (0, 0)), pl.BlockSpec(w_in.shape, lambda t: (0, 0))],
        out_specs=tuple(out for _ in range(n)),
        compiler_params=pltpu.CompilerParams(dimension_semantics=("parallel",), vmem_limit_bytes=VMEM_LIMIT),
        name="in_projection",
    )(x, norm_g.reshape(1, D), w_in.astype(MM_DTYPE))


def _out_proj_kernel(a_ref, r_ref, w_ref, x_ref, o_ref):
    o = jnp.dot(a_ref[...], w_ref[:NA_WIDTH, :], preferred_element_type=jnp.float32)
    o = o + jnp.dot(r_ref[...], w_ref[NA_WIDTH:, :], preferred_element_type=jnp.float32)
    o_ref[...] = x_ref[...] + o


def out_projection(na_o, ret_o, w_out, x):
    T, D = x.shape
    tok = pl.BlockSpec((PROJ_TOKENS, D), lambda t: (t, 0))
    half = pl.BlockSpec((PROJ_TOKENS, NA_WIDTH), lambda t: (t, 0))
    return pl.pallas_call(
        _out_proj_kernel,
        out_shape=jax.ShapeDtypeStruct((T, D), jnp.float32),
        grid=(T // PROJ_TOKENS,),
        in_specs=[half, half, pl.BlockSpec(w_out.shape, lambda t: (0, 0)), tok],
        out_specs=tok,
        compiler_params=pltpu.CompilerParams(dimension_semantics=("parallel",), vmem_limit_bytes=VMEM_LIMIT),
        name="out_projection",
    )(na_o, ret_o, w_out.astype(MM_DTYPE), x)


def encoder_layer(x, norm1_g, w_in, na_rpb, ret_decay_fwd, ret_decay_bwd, ret_norm_g,
                  w_out, norm2_g, peer_w_q, peer_sub_keys, peer_u, peer_v, final_g):
    B, L, D = x.shape
    xt = x.reshape(B * L, D)
    streams = [s.reshape(B, L, NA_WIDTH) for s in in_projection(xt, norm1_g, w_in)]
    na_o = neighbourhood_attention_pallas(*streams[:N_NA_STREAMS], na_rpb)
    ret_o = retention_pallas(*streams[N_NA_STREAMS:], ret_decay_fwd, ret_decay_bwd, ret_norm_g)
    x1 = out_projection(na_o.reshape(B * L, NA_WIDTH), ret_o.reshape(B * L, RET_WIDTH), w_out, xt)
    y = peer_and_final_norm(x1, norm2_g, peer_w_q, peer_sub_keys, peer_u, peer_v, final_g)
    return y.reshape(B, L, D)


def kernel(x_prompt, x_sample, norm1_g, w_in, na_rpb, ret_decay_fwd, ret_decay_bwd, ret_norm_g,
           w_out, norm2_g, peer_w_q, peer_sub_keys, peer_u, peer_v, final_g):
    assert DEPTH == 1

    def trunk(x):
        return encoder_layer(x, norm1_g[0], w_in[0], na_rpb[0], ret_decay_fwd[0], ret_decay_bwd[0],
                             ret_norm_g[0], w_out[0], norm2_g[0], peer_w_q[0], peer_sub_keys[0],
                             peer_u[0], peer_v[0], final_g)

    return (trunk(x_prompt), trunk(x_sample))
```

```python
import functools
import math
import jax, jax.numpy as jnp
from jax import lax
from jax.experimental import pallas as pl
from jax.experimental.pallas import tpu as pltpu

D_MODEL = 1024
DEPTH = 1
GRID_W = 64
NA_HEADS = 8
NA_HEAD_DIM = 64
NA_WIDTH = NA_HEADS * NA_HEAD_DIM
NA_ROWS = 8
NA_COLS = 16
RET_HEADS = 4
RET_HEAD_DIM = 128
RET_WIDTH = RET_HEADS * RET_HEAD_DIM
RET_CHUNK = 128
ROPE_BASE = 10000.0
PEER_HEADS = 8
PEER_QDIM = 256
PEER_HALF = PEER_QDIM // 2
N_KEYS = 128
N_EXPERTS = N_KEYS * N_KEYS
PEER_TOPK = 16
EPS = 1e-6

LANES = 128
SUBLANES = 8
VMEM_LIMIT = 56 * 1024 * 1024
MM_DTYPE = jnp.bfloat16


def _dot_nt(a, b):
    return lax.dot_general(a, b, (((1,), (1,)), ((), ())), preferred_element_type=jnp.float32)


def _dot_tn(a, b):
    return lax.dot_general(a, b, (((0,), (0,)), ((), ())), preferred_element_type=jnp.float32)


GATE_DTYPE = jnp.bfloat16
GATE_PACK = 2
NOT_TOP = 255.0
SEL_TOKENS = 1024
MIX_TOKENS = 512
MIX_KEYS = 16


def _topk_rank(s):
    iota = lax.broadcasted_iota(jnp.int32, s.shape, 0).astype(jnp.float32)
    rank = jnp.full(s.shape, NOT_TOP, jnp.float32)
    tops = []
    for r in range(PEER_TOPK):
        m = jnp.max(s, axis=0, keepdims=True)
        idx = jnp.min(jnp.where(s == m, iota, float(s.shape[0])), axis=0, keepdims=True)
        hit = iota == idx
        rank = jnp.where(hit, float(r), rank)
        s = jnp.where(hit, -jnp.inf, s)
        tops.append(m)
    return rank, tops


def _topk_rank_untied(s):
    rank = jnp.full(s.shape, NOT_TOP, jnp.float32)
    tops = []
    for r in range(PEER_TOPK):
        m = jnp.max(s, axis=0, keepdims=True)
        hit = s == m
        rank = jnp.where(hit, float(r), rank)
        s = jnp.where(hit, -jnp.inf, s)
        tops.append(m)
    removed = jnp.sum(jnp.where(rank < NOT_TOP, 1.0, 0.0), axis=0, keepdims=True)
    return rank, tops, removed


def _young_counts(a_top, b_top):
    k = PEER_TOPK
    w = a_top[0].shape[1]
    b16 = jnp.concatenate(b_top, axis=0)
    b8 = jnp.concatenate(b_top[:8], axis=0)
    blocks = [a_top[0] + b16] + [a_top[a] + b8 for a in range(1, 8)]
    blocks.append(jnp.concatenate(a_top[8:], axis=0) + b_top[0])
    cand = jnp.concatenate(blocks, axis=0)
    rows = cand.shape[0]
    rho = lax.broadcasted_iota(jnp.int32, (rows, w), 0)
    mid = rho - 16
    a_id = jnp.where(rho < 16, 0, jnp.where(rho < 72, 1 + (mid >> 3), rho - 64))
    b_id = jnp.where(rho < 16, rho, jnp.where(rho < 72, mid & 7, 0))
    pos = (a_id * k + b_id).astype(jnp.float32)
    cand = jnp.where((a_id + 1) * (b_id + 1) <= k, cand, -jnp.inf)
    a_iota = lax.broadcasted_iota(jnp.int32, (k, w), 0).astype(jnp.float32)
    n = jnp.zeros((k, w), jnp.float32)
    best = []
    for _ in range(k):
        m = jnp.max(cand, axis=0, keepdims=True)
        p = jnp.min(jnp.where(cand == m, pos, float(k * k)), axis=0, keepdims=True)
        cand = jnp.where(pos == p, -jnp.inf, cand)
        n = n + (a_iota == jnp.floor(p * (1.0 / k))).astype(jnp.float32)
        best.append(m)
    return n, best


def _peer_select_kernel(x_ref, g_ref, wq_ref, keys_ref, h_ref, r1_ref, e1_ref, n0_ref, c0_ref, hb_ref):
    hd = pl.program_id(1)

    @pl.when(hd == 0)
    def _():
        x = x_ref[...]
        y = x * lax.rsqrt(jnp.mean(x * x, axis=-1, keepdims=True) + EPS) * g_ref[...]
        hb_ref[...] = y.astype(hb_ref.dtype)
        h_ref[...] = y.astype(h_ref.dtype)

    def emit(c, s0, s1, rank0, a_top, rank1, b_top):
        n, best = _young_counts(a_top, b_top)
        z = jnp.zeros_like(best[0])
        for m in best:
            z = z + jnp.exp(m - best[0])
        n0 = jnp.zeros_like(s0)
        for a in range(PEER_TOPK):
            n0 = jnp.where(rank0 == float(a), n[a:a + 1, :], n0)
        sl = slice(c * LANES, (c + 1) * LANES)
        e1 = jnp.where(rank1 < NOT_TOP, jnp.exp(s1 - b_top[0]), 0.0)
        r1_ref[:, sl] = pltpu.bitcast(rank1.astype(GATE_DTYPE), r1_ref.dtype)
        e1_ref[:, sl] = pltpu.bitcast(e1.astype(GATE_DTYPE), e1_ref.dtype)
        n0_ref[:, sl] = n0
        c0_ref[:, sl] = jnp.exp(s0 - a_top[0]) * (0.5 / z)

    chunks = range(SEL_TOKENS // LANES)
    scores = []
    qs = [jnp.dot(hb_ref[c * LANES:(c + 1) * LANES, :], wq_ref[...],
                  preferred_element_type=jnp.float32).astype(keys_ref.dtype) for c in chunks]
    for qc in qs:
        scores.append((_dot_nt(keys_ref[0, 0], qc[:, :PEER_HALF]),
                       _dot_nt(keys_ref[0, 1], qc[:, PEER_HALF:])))
    untied = [(_topk_rank_untied(s0), _topk_rank_untied(s1)) for s0, s1 in scores]
    removed = functools.reduce(jnp.maximum, [t[2] for pair in untied for t in pair])
    tied = jnp.max(removed) > float(PEER_TOPK)

    @pl.when(tied)
    def _():
        for c, (s0, s1) in zip(chunks, scores):
            emit(c, s0, s1, *_topk_rank(s0), *_topk_rank(s1))

    @pl.when(jnp.logical_not(tied))
    def _():
        for c, (s0, s1), (t0, t1) in zip(chunks, scores, untied):
            emit(c, s0, s1, t0[0], t0[1], t1[0], t1[1])


def _twice_gelu_tanh(x):
    c = math.sqrt(2.0 / math.pi)
    return x * (1.0 + jnp.tanh(x * (c + (c * 0.044715) * (x * x))))


def _peer_mix_kernel(h_ref, u_ref, vt_ref, r1_ref, e1_ref, n0_ref, c0_ref, x_ref, fg_ref, y_ref,
                     acc_ref, a_ref, w_ref):
    e = pl.program_id(1)

    @pl.when(e == 0)
    def _():
        acc_ref[...] = jnp.zeros_like(acc_ref)

    gd = GATE_DTYPE
    pack = SUBLANES * GATE_PACK
    krows = N_KEYS // GATE_PACK
    half = MIX_TOKENS // 2
    for hf in range(2):
        a_ref[:, hf * half:(hf + 1) * half] = _dot_nt(u_ref[...], h_ref[hf * half:(hf + 1) * half, :])
    rows = [pl.ds(pl.multiple_of(hd * N_KEYS + e * MIX_KEYS, MIX_KEYS), MIX_KEYS) for hd in range(PEER_HEADS)]
    for hf in range(2):
        for c in range(hf * half // LANES, (hf + 1) * half // LANES):
            sl = slice(c * LANES, (c + 1) * LANES)
            n_t = [n0_ref[r, sl] for r in rows]
            c_t = [c0_ref[r, sl] for r in rows]
            for ii in range(MIX_KEYS):
                g = jnp.zeros((N_KEYS // pack, pack, LANES), gd)
                for hd in range(PEER_HEADS):
                    n_b = jnp.broadcast_to(n_t[hd][ii:ii + 1, :], (pack, LANES)).astype(gd)[None]
                    c_b = jnp.broadcast_to(c_t[hd][ii:ii + 1, :], (pack, LANES)).astype(gd)[None]
                    r1 = pltpu.bitcast(r1_ref[hd * krows:(hd + 1) * krows, sl], gd).reshape(g.shape)
                    e1 = pltpu.bitcast(e1_ref[hd * krows:(hd + 1) * krows, sl], gd).reshape(g.shape)
                    g = g + jnp.where(r1 < n_b, e1, jnp.zeros_like(e1)) * c_b
                act = _twice_gelu_tanh(a_ref[ii * N_KEYS:(ii + 1) * N_KEYS, sl])
                gate = g.reshape(N_KEYS, LANES).astype(jnp.float32)
                w_ref[ii * N_KEYS:(ii + 1) * N_KEYS, sl] = (act * gate).astype(w_ref.dtype)
        hs = slice(hf * half, (hf + 1) * half)
        acc_ref[:, hs] += jnp.dot(vt_ref[...], w_ref[:, hs], preferred_element_type=jnp.float32)

    @pl.when(e == pl.num_programs(1) - 1)
    def _():
        o = acc_ref[...].T + x_ref[...]
        y_ref[...] = o * lax.rsqrt(jnp.mean(o * o, axis=-1, keepdims=True) + EPS) * fg_ref[...]


def peer_and_final_norm(x, norm2_g, w_q, sub_keys, u, v, final_g):
    T, D = x.shape
    assert T % MIX_TOKENS == 0 and T % SEL_TOKENS == 0 and N_KEYS % MIX_KEYS == 0 and MIX_KEYS % SUBLANES == 0
    f32 = jnp.float32
    tab = jax.ShapeDtypeStruct((PEER_HEADS * N_KEYS, T), f32)
    gtab = jax.ShapeDtypeStruct((PEER_HEADS * N_KEYS // GATE_PACK, T), jnp.uint32)
    tab_spec = pl.BlockSpec((N_KEYS, SEL_TOKENS), lambda t, h: (h, t))
    gtab_spec = pl.BlockSpec((N_KEYS // GATE_PACK, SEL_TOKENS), lambda t, h: (h, t))
    h, r1, e1, n0, c0 = pl.pallas_call(
        _peer_select_kernel,
        out_shape=(jax.ShapeDtypeStruct((T, D), MM_DTYPE), gtab, gtab, tab, tab),
        grid=(T // SEL_TOKENS, PEER_HEADS),
        in_specs=[pl.BlockSpec((SEL_TOKENS, D), lambda t, h: (t, 0)),
                  pl.BlockSpec((1, D), lambda t, h: (0, 0)),
                  pl.BlockSpec((D, PEER_QDIM), lambda t, h: (0, h)),
                  pl.BlockSpec((1, 2, N_KEYS, PEER_HALF), lambda t, h: (h, 0, 0, 0))],
        out_specs=(pl.BlockSpec((SEL_TOKENS, D), lambda t, h: (t, 0)), gtab_spec, gtab_spec, tab_spec, tab_spec),
        scratch_shapes=[pltpu.VMEM((SEL_TOKENS, D), MM_DTYPE)],
        compiler_params=pltpu.CompilerParams(dimension_semantics=("parallel", "arbitrary"),
                                             vmem_limit_bytes=VMEM_LIMIT),
        name="peer_select",
    )(x, norm2_g.reshape(1, D), w_q.astype(MM_DTYPE), sub_keys.astype(MM_DTYPE))

    eb = MIX_KEYS * N_KEYS
    mix_tab = pl.BlockSpec((PEER_HEADS * N_KEYS, MIX_TOKENS), lambda t, e: (0, t))
    mix_gtab = pl.BlockSpec((PEER_HEADS * N_KEYS // GATE_PACK, MIX_TOKENS), lambda t, e: (0, t))
    tok_blk = pl.BlockSpec((MIX_TOKENS, D), lambda t, e: (t, 0))
    return pl.pallas_call(
        _peer_mix_kernel,
        out_shape=jax.ShapeDtypeStruct((T, D), f32),
        grid=(T // MIX_TOKENS, N_KEYS // MIX_KEYS),
        in_specs=[tok_blk,
                  pl.BlockSpec((eb, D), lambda t, e: (e, 0)),
                  pl.BlockSpec((D, eb), lambda t, e: (0, e)),
                  mix_gtab, mix_gtab, mix_tab, mix_tab,
                  tok_blk,
                  pl.BlockSpec((1, D), lambda t, e: (0, 0))],
        out_specs=tok_blk,
        scratch_shapes=[pltpu.VMEM((D, MIX_TOKENS), f32),
                        pltpu.VMEM((eb, MIX_TOKENS), f32),
                        pltpu.VMEM((eb, MIX_TOKENS), MM_DTYPE)],
        compiler_params=pltpu.CompilerParams(dimension_semantics=("parallel", "arbitrary"),
                                             vmem_limit_bytes=VMEM_LIMIT),
        name="peer_mix",
    )(h, u.astype(MM_DTYPE), v.T.astype(MM_DTYPE), r1, e1, n0, c0, x, final_g.reshape(1, D))


NA_BAND = NA_ROWS * GRID_W
NA_MASKED = -1e30
NA_STEP_ROWS = 2


def _na_bias_table(rpb):
    cq = jnp.arange(GRID_W)[:, None]
    ck = jnp.arange(GRID_W)[None, :]
    cs = jnp.clip(cq - NA_COLS // 2, 0, GRID_W - NA_COLS)
    valid = (ck >= cs) & (ck < cs + NA_COLS)
    cb = jnp.clip(ck - cq + NA_COLS - 1, 0, 2 * NA_COLS - 2)
    rb = jnp.arange(NA_ROWS)[:, None] + jnp.arange(NA_ROWS)[None, :]
    tab = rpb[:, rb][:, :, :, cb]
    tab = jnp.where(valid, tab, NA_MASKED)
    return jnp.transpose(tab, (1, 0, 3, 2, 4)).reshape(NA_ROWS, NA_HEADS, GRID_W, NA_BAND)


def _na_row_start(r, rows):
    return jnp.clip(r - NA_ROWS // 2, 0, rows - NA_ROWS)


def _na_kernel(q_ref, k_ref, v_ref, *rest, rows):
    tab_refs, o_ref = rest[:NA_STEP_ROWS], rest[NA_STEP_ROWS]
    lane = lax.broadcasted_iota(jnp.int32, (GRID_W, LANES), 1)
    lo = lane < NA_HEAD_DIM
    pairs = [slice(hp * LANES, (hp + 1) * LANES) for hp in range(NA_HEADS // 2)]
    bands, scores = [], []
    for j in range(NA_STEP_ROWS):
        r = pl.program_id(1) * NA_STEP_ROWS + j
        start = pl.multiple_of(_na_row_start(r, rows) * GRID_W, GRID_W)
        kb = k_ref[0, pl.ds(start, NA_BAND), :]
        bands.append(v_ref[0, pl.ds(start, NA_BAND), :])
        q = q_ref[0, j * GRID_W:(j + 1) * GRID_W, :]
        for h in range(NA_HEADS):
            qp = q[:, pairs[h // 2]]
            mine = lo if h % 2 == 0 else jnp.logical_not(lo)
            scores.append(_dot_nt(jnp.where(mine, qp, jnp.zeros_like(qp)), kb[:, pairs[h // 2]])
                          + tab_refs[j][0, h])
    probs, sums = [], []
    for sc in scores:
        p = jnp.exp(sc - jnp.max(sc, axis=-1, keepdims=True))
        sums.append(jnp.sum(p, axis=-1, keepdims=True))
        probs.append(p.astype(bands[0].dtype))
    for j in range(NA_STEP_ROWS):
        outs = []
        for hp in range(NA_HEADS // 2):
            i0 = j * NA_HEADS + 2 * hp
            o = [jnp.dot(probs[i0 + s], bands[j][:, pairs[hp]], preferred_element_type=jnp.float32) / sums[i0 + s]
                 for s in range(2)]
            outs.append(jnp.where(lo, o[0], o[1]))
        o_ref[0, j * GRID_W:(j + 1) * GRID_W, :] = jnp.concatenate(outs, axis=-1).astype(o_ref.dtype)


def neighbourhood_attention_pallas(q, k, v, rpb):
    B, L, W = q.shape
    rows = L // GRID_W
    assert rows >= NA_ROWS and rows % NA_STEP_ROWS == 0 and W == NA_HEADS * NA_HEAD_DIM and 2 * NA_HEAD_DIM == LANES
    seq = pl.BlockSpec((1, L, W), lambda b, r: (b, 0, 0))
    row = pl.BlockSpec((1, NA_STEP_ROWS * GRID_W, W), lambda b, r: (b, r, 0))

    def tab(j):
        def index(b, r):
            rr = r * NA_STEP_ROWS + j
            return (_na_row_start(rr, rows) - rr + NA_ROWS - 1, 0, 0, 0)
        return pl.BlockSpec((1, NA_HEADS, GRID_W, NA_BAND), index)

    bias = _na_bias_table(rpb)
    return pl.pallas_call(
        functools.partial(_na_kernel, rows=rows),
        out_shape=jax.ShapeDtypeStruct((B, L, W), MM_DTYPE),
        grid=(B, rows // NA_STEP_ROWS),
        in_specs=[row, seq, seq] + [tab(j) for j in range(NA_STEP_ROWS)],
        out_specs=row,
        compiler_params=pltpu.CompilerParams(dimension_semantics=("parallel", "arbitrary"),
                                             vmem_limit_bytes=VMEM_LIMIT),
        name="na_attention",
    )(q, k, v, *([bias] * NA_STEP_ROWS))


def _rotate_half(x, cos, sin_signed):
    return x * cos + pltpu.roll(x, x.shape[-1] // 2, axis=1) * sin_signed


def _ret_kernel(q_ref, k_ref, v_ref, g_ref, cos_ref, sin_ref, df_ref, db_ref, ng_ref, o_ref,
                of_ref, ob_ref, *, nc):
    C = RET_CHUNK
    mm = o_ref.dtype
    lg_f = jax.nn.log_sigmoid(df_ref[0])[0:1, :]
    lg_b = jax.nn.log_sigmoid(db_ref[0])[0:1, :]
    pos = lax.broadcasted_iota(jnp.int32, (C, C), 0).astype(jnp.float32)
    diff = pos - lax.broadcasted_iota(jnp.int32, (C, C), 1).astype(jnp.float32)
    dec_f = jnp.where(diff >= 0, jnp.exp(lg_f * jnp.maximum(diff, 0.0)), 0.0)
    dec_b = jnp.where(diff < 0, jnp.exp(lg_b * jnp.maximum(-diff, 0.0)), 0.0)
    kdec_f = jnp.exp(lg_f * (C - 1 - pos))
    qdec_f = jnp.exp(lg_f * (pos + 1))
    kdec_b = jnp.exp(lg_b * pos)
    qdec_b = jnp.exp(lg_b * (C - pos))
    step_f = jnp.exp(lg_f * C)
    step_b = jnp.exp(lg_b * C)
    scale = RET_HEAD_DIM ** -0.5

    def load(c):
        sl = pl.ds(pl.multiple_of(c * C, C), C)
        q = _rotate_half(q_ref[0, sl, :], cos_ref[sl, :], sin_ref[sl, :])
        k = _rotate_half(k_ref[0, sl, :], cos_ref[sl, :], sin_ref[sl, :]) * scale
        return sl, q, k, v_ref[0, sl, :].astype(mm)

    def body(c, carry):
        rf, rb = carry
        sl_f, q_f, k_f, v_f = load(c)
        sl_b, q_b, k_b, v_b = load(nc - 1 - c)
        att_f = _dot_nt(q_f.astype(mm), k_f.astype(mm))
        att_b = _dot_nt(q_b.astype(mm), k_b.astype(mm))
        inter_f = jnp.dot((q_f * qdec_f).astype(mm), rf.astype(mm), preferred_element_type=jnp.float32)
        inter_b = jnp.dot((q_b * qdec_b).astype(mm), rb.astype(mm), preferred_element_type=jnp.float32)
        s_f = _dot_tn((k_f * kdec_f).astype(mm), v_f)
        s_b = _dot_tn((k_b * kdec_b).astype(mm), v_b)
        of_ref[sl_f, :] = inter_f + jnp.dot((att_f * dec_f).astype(mm), v_f, preferred_element_type=jnp.float32)
        ob_ref[sl_b, :] = inter_b + jnp.dot((att_b * dec_b).astype(mm), v_b, preferred_element_type=jnp.float32)
        return step_f * rf + s_f, step_b * rb + s_b

    zero = jnp.zeros((RET_HEAD_DIM, RET_HEAD_DIM), jnp.float32)
    lax.fori_loop(0, nc, body, (zero, zero))

    def finish(c, carry):
        sl = pl.ds(pl.multiple_of(c * C, C), C)
        o = of_ref[sl, :] + ob_ref[sl, :]
        mu = jnp.mean(o, axis=-1, keepdims=True)
        var = jnp.mean(jnp.square(o - mu), axis=-1, keepdims=True)
        o = (o - mu) * lax.rsqrt(var + EPS) * ng_ref[...]
        o_ref[0, sl, :] = (jax.nn.silu(g_ref[0, sl, :]) * o).astype(mm)
        return carry

    lax.fori_loop(0, nc, finish, 0)


def retention_pallas(q, k, v, gate, decay_fwd, decay_bwd, norm_g):
    B, L, W = q.shape
    dh = RET_HEAD_DIM
    assert W == RET_HEADS * dh and dh == LANES and L % RET_CHUNK == 0
    inv = 1.0 / (ROPE_BASE ** (jnp.arange(0, dh, 2, dtype=jnp.float32) / dh))
    ang = jnp.arange(L, dtype=jnp.float32)[:, None] * inv[None, :]
    cos = jnp.concatenate([jnp.cos(ang), jnp.cos(ang)], axis=-1)
    sin = jnp.concatenate([-jnp.sin(ang), jnp.sin(ang)], axis=-1)
    bcast = lambda d: jnp.broadcast_to(d.astype(jnp.float32)[:, None, None], (RET_HEADS, SUBLANES, dh))
    seq = pl.BlockSpec((1, L, dh), lambda b, h: (b, 0, h))
    tab = pl.BlockSpec((L, dh), lambda b, h: (0, 0))
    dec = pl.BlockSpec((1, SUBLANES, dh), lambda b, h: (h, 0, 0))
    return pl.pallas_call(
        functools.partial(_ret_kernel, nc=L // RET_CHUNK),
        out_shape=jax.ShapeDtypeStruct((B, L, W), MM_DTYPE),
        grid=(B, RET_HEADS),
        in_specs=[seq, seq, seq, seq, tab, tab, dec, dec, pl.BlockSpec((1, dh), lambda b, h: (0, h))],
        out_specs=seq,
        scratch_shapes=[pltpu.VMEM((L, dh), jnp.float32), pltpu.VMEM((L, dh), jnp.float32)],
        compiler_params=pltpu.CompilerParams(dimension_semantics=("parallel", "parallel"),
                                             vmem_limit_bytes=VMEM_LIMIT),
        name="retention",
    )(q, k, v, gate, cos, sin, bcast(decay_fwd), bcast(decay_bwd), norm_g.reshape(1, W))


PROJ_TOKENS = 512
N_NA_STREAMS = 3
N_RET_STREAMS = 4


def _in_proj_kernel(x_ref, g_ref, w_ref, *out_refs):
    x = x_ref[...]
    h = (x * lax.rsqrt(jnp.mean(x * x, axis=-1, keepdims=True) + EPS) * g_ref[...]).astype(w_ref.dtype)
    for j, o_ref in enumerate(out_refs):
        o = jnp.dot(h, w_ref[:, j * NA_WIDTH:(j + 1) * NA_WIDTH], preferred_element_type=jnp.float32)
        if j == 0:
            o = o * NA_HEAD_DIM ** -0.5
        o_ref[...] = o.astype(o_ref.dtype)


def in_projection(x, norm_g, w_in):
    T, D = x.shape
    assert NA_WIDTH == RET_WIDTH and T % PROJ_TOKENS == 0
    n = N_NA_STREAMS + N_RET_STREAMS
    assert w_in.shape == (D, n * NA_WIDTH)
    tok = pl.BlockSpec((PROJ_TOKENS, D), lambda t: (t, 0))
    out = pl.BlockSpec((PROJ_TOKENS, NA_WIDTH), lambda t: (t, 0))
    return pl.pallas_call(
        _in_proj_kernel,
        out_shape=tuple(jax.ShapeDtypeStruct((T, NA_WIDTH), MM_DTYPE if j < N_NA_STREAMS else jnp.float32)
                        for j in range(n)),
        grid=(T // PROJ_TOKENS,),
        in_specs=[tok, pl.BlockSpec((1, D), lambda t: (0, 0)), pl.BlockSpec(w_in.shape, lambda t: (0, 0))],
        out_specs=tuple(out for _ in range(n)),
        compiler_params=pltpu.CompilerParams(dimension_semantics=("parallel",), vmem_limit_bytes=VMEM_LIMIT),
        name="in_projection",
    )(x, norm_g.reshape(1, D), w_in.astype(MM_DTYPE))


def _out_proj_kernel(a_ref, r_ref, w_ref, x_ref, o_ref):
    o = jnp.dot(a_ref[...], w_ref[:NA_WIDTH, :], preferred_element_type=jnp.float32)
    o = o + jnp.dot(r_ref[...], w_ref[NA_WIDTH:, :], preferred_element_type=jnp.float32)
    o_ref[...] = x_ref[...] + o


def out_projection(na_o, ret_o, w_out, x):
    T, D = x.shape
    tok = pl.BlockSpec((PROJ_TOKENS, D), lambda t: (t, 0))
    half = pl.BlockSpec((PROJ_TOKENS, NA_WIDTH), lambda t: (t, 0))
    return pl.pallas_call(
        _out_proj_kernel,
        out_shape=jax.ShapeDtypeStruct((T, D), jnp.float32),
        grid=(T // PROJ_TOKENS,),
        in_specs=[half, half, pl.BlockSpec(w_out.shape, lambda t: (0, 0)), tok],
        out_specs=tok,
        compiler_params=pltpu.CompilerParams(dimension_semantics=("parallel",), vmem_limit_bytes=VMEM_LIMIT),
        name="out_projection",
    )(na_o, ret_o, w_out.astype(MM_DTYPE), x)


def encoder_layer(x, norm1_g, w_in, na_rpb, ret_decay_fwd, ret_decay_bwd, ret_norm_g,
                  w_out, norm2_g, peer_w_q, peer_sub_keys, peer_u, peer_v, final_g):
    B, L, D = x.shape
    xt = x.reshape(B * L, D)
    streams = [s.reshape(B, L, NA_WIDTH) for s in in_projection(xt, norm1_g, w_in)]
    na_o = neighbourhood_attention_pallas(*streams[:N_NA_STREAMS], na_rpb)
    ret_o = retention_pallas(*streams[N_NA_STREAMS:], ret_decay_fwd, ret_decay_bwd, ret_norm_g)
    x1 = out_projection(na_o.reshape(B * L, NA_WIDTH), ret_o.reshape(B * L, RET_WIDTH), w_out, xt)
    y = peer_and_final_norm(x1, norm2_g, peer_w_q, peer_sub_keys, peer_u, peer_v, final_g)
    return y.reshape(B, L, D)


def kernel(x_prompt, x_sample, norm1_g, w_in, na_rpb, ret_decay_fwd, ret_decay_bwd, ret_norm_g,
           w_out, norm2_g, peer_w_q, peer_sub_keys, peer_u, peer_v, final_g):
    assert DEPTH == 1

    def trunk(x):
        return encoder_layer(x, norm1_g[0], w_in[0], na_rpb[0], ret_decay_fwd[0], ret_decay_bwd[0],
                             ret_norm_g[0], w_out[0], norm2_g[0], peer_w_q[0], peer_sub_keys[0],
                             peer_u[0], peer_v[0], final_g)

    return (trunk(x_prompt), trunk(x_sample))
```

```python
import functools
import math
import jax, jax.numpy as jnp
from jax import lax
from jax.experimental import pallas as pl
from jax.experimental.pallas import tpu as pltpu

D_MODEL = 1024
DEPTH = 1
GRID_W = 64
NA_HEADS = 8
NA_HEAD_DIM = 64
NA_WIDTH = NA_HEADS * NA_HEAD_DIM
NA_ROWS = 8
NA_COLS = 16
RET_HEADS = 4
RET_HEAD_DIM = 128
RET_WIDTH = RET_HEADS * RET_HEAD_DIM
RET_CHUNK = 128
ROPE_BASE = 10000.0
PEER_HEADS = 8
PEER_QDIM = 256
PEER_HALF = PEER_QDIM // 2
N_KEYS = 128
N_EXPERTS = N_KEYS * N_KEYS
PEER_TOPK = 16
EPS = 1e-6

LANES = 128
SUBLANES = 8
VMEM_LIMIT = 56 * 1024 * 1024
MM_DTYPE = jnp.bfloat16


def _dot_nt(a, b):
    return lax.dot_general(a, b, (((1,), (1,)), ((), ())), preferred_element_type=jnp.float32)


def _dot_tn(a, b):
    return lax.dot_general(a, b, (((0,), (0,)), ((), ())), preferred_element_type=jnp.float32)


GATE_DTYPE = jnp.bfloat16
GATE_PACK = 2
NOT_TOP = 255.0
SEL_TOKENS = 1024
MIX_TOKENS = 512
MIX_KEYS = 16


def _topk_rank(s):
    iota = lax.broadcasted_iota(jnp.int32, s.shape, 0).astype(jnp.float32)
    rank = jnp.full(s.shape, NOT_TOP, jnp.float32)
    tops = []
    for r in range(PEER_TOPK):
        m = jnp.max(s, axis=0, keepdims=True)
        idx = jnp.min(jnp.where(s == m, iota, float(s.shape[0])), axis=0, keepdims=True)
        hit = iota == idx
        rank = jnp.where(hit, float(r), rank)
        s = jnp.where(hit, -jnp.inf, s)
        tops.append(m)
    return rank, tops


def _topk_rank_untied(s):
    rank = jnp.full(s.shape, NOT_TOP, jnp.float32)
    tops = []
    for r in range(PEER_TOPK):
        m = jnp.max(s, axis=0, keepdims=True)
        hit = s == m
        rank = jnp.where(hit, float(r), rank)
        s = jnp.where(hit, -jnp.inf, s)
        tops.append(m)
    removed = jnp.sum(jnp.where(rank < NOT_TOP, 1.0, 0.0), axis=0, keepdims=True)
    return rank, tops, removed


def _young_counts(a_top, b_top):
    k = PEER_TOPK
    w = a_top[0].shape[1]
    b16 = jnp.concatenate(b_top, axis=0)
    b8 = jnp.concatenate(b_top[:8], axis=0)
    blocks = [a_top[0] + b16] + [a_top[a] + b8 for a in range(1, 8)]
    blocks.append(jnp.concatenate(a_top[8:], axis=0) + b_top[0])
    cand = jnp.concatenate(blocks, axis=0)
    rows = cand.shape[0]
    rho = lax.broadcasted_iota(jnp.int32, (rows, w), 0)
    mid = rho - 16
    a_id = jnp.where(rho < 16, 0, jnp.where(rho < 72, 1 + (mid >> 3), rho - 64))
    b_id = jnp.where(rho < 16, rho, jnp.where(rho < 72, mid & 7, 0))
    pos = (a_id * k + b_id).astype(jnp.float32)
    cand = jnp.where((a_id + 1) * (b_id + 1) <= k, cand, -jnp.inf)
    a_iota = lax.broadcasted_iota(jnp.int32, (k, w), 0).astype(jnp.float32)
    n = jnp.zeros((k, w), jnp.float32)
    best = []
    for _ in range(k):
        m = jnp.max(cand, axis=0, keepdims=True)
        p = jnp.min(jnp.where(cand == m, pos, float(k * k)), axis=0, keepdims=True)
        cand = jnp.where(pos == p, -jnp.inf, cand)
        n = n + (a_iota == jnp.floor(p * (1.0 / k))).astype(jnp.float32)
        best.append(m)
    return n, best


def _peer_select_kernel(x_ref, g_ref, wq_ref, keys_ref, h_ref, r1_ref, e1_ref, n0_ref, c0_ref, hb_ref):
    hd = pl.program_id(1)

    @pl.when(hd == 0)
    def _():
        x = x_ref[...]
        y = x * lax.rsqrt(jnp.mean(x * x, axis=-1, keepdims=True) + EPS) * g_ref[...]
        hb_ref[...] = y.astype(hb_ref.dtype)
        h_ref[...] = y.astype(h_ref.dtype)

    def emit(c, s0, s1, rank0, a_top, rank1, b_top):
        n, best = _young_counts(a_top, b_top)
        z = jnp.zeros_like(best[0])
        for m in best:
            z = z + jnp.exp(m - best[0])
        n0 = jnp.zeros_like(s0)
        for a in range(PEER_TOPK):
            n0 = jnp.where(rank0 == float(a), n[a:a + 1, :], n0)
        sl = slice(c * LANES, (c + 1) * LANES)
        e1 = jnp.where(rank1 < NOT_TOP, jnp.exp(s1 - b_top[0]), 0.0)
        r1_ref[:, sl] = pltpu.bitcast(rank1.astype(GATE_DTYPE), r1_ref.dtype)
        e1_ref[:, sl] = pltpu.bitcast(e1.astype(GATE_DTYPE), e1_ref.dtype)
        n0_ref[:, sl] = n0
        c0_ref[:, sl] = jnp.exp(s0 - a_top[0]) * (0.5 / z)

    chunks = range(SEL_TOKENS // LANES)
    scores = []
    qs = [jnp.dot(hb_ref[c * LANES:(c + 1) * LANES, :], wq_ref[...],
                  preferred_element_type=jnp.float32).astype(keys_ref.dtype) for c in chunks]
    for qc in qs:
        scores.append((_dot_nt(keys_ref[0, 0], qc[:, :PEER_HALF]),
                       _dot_nt(keys_ref[0, 1], qc[:, PEER_HALF:])))
    untied = [(_topk_rank_untied(s0), _topk_rank_untied(s1)) for s0, s1 in scores]
    removed = functools.reduce(jnp.maximum, [t[2] for pair in untied for t in pair])
    tied = jnp.max(removed) > float(PEER_TOPK)

    @pl.when(tied)
    def _():
        for c, (s0, s1) in zip(chunks, scores):
            emit(c, s0, s1, *_topk_rank(s0), *_topk_rank(s1))

    @pl.when(jnp.logical_not(tied))
    def _():
        for c, (s0, s1), (t0, t1) in zip(chunks, scores, untied):
            emit(c, s0, s1, t0[0], t0[1], t1[0], t1[1])


def _twice_gelu_tanh(x):
    c = math.sqrt(2.0 / math.pi)
    return x * (1.0 + jnp.tanh(x * (c + (c * 0.044715) * (x * x))))


def _peer_mix_kernel(h_ref, u_ref, vt_ref, r1_ref, e1_ref, n0_ref, c0_ref, x_ref, fg_ref, y_ref,
                     acc_ref, a_ref, w_ref):
    e = pl.program_id(1)

    @pl.when(e == 0)
    def _():
        acc_ref[...] = jnp.zeros_like(acc_ref)

    gd = GATE_DTYPE
    pack = SUBLANES * GATE_PACK
    krows = N_KEYS // GATE_PACK
    half = MIX_TOKENS // 2
    for hf in range(2):
        a_ref[:, hf * half:(hf + 1) * half] = _dot_nt(u_ref[...], h_ref[hf * half:(hf + 1) * half, :])
    rows = [pl.ds(pl.multiple_of(hd * N_KEYS + e * MIX_KEYS, MIX_KEYS), MIX_KEYS) for hd in range(PEER_HEADS)]
    for hf in range(2):
        for c in range(hf * half // LANES, (hf + 1) * half // LANES):
            sl = slice(c * LANES, (c + 1) * LANES)
            n_t = [n0_ref[r, sl] for r in rows]
            c_t = [c0_ref[r, sl] for r in rows]
            for ii in range(MIX_KEYS):
                g = None
                tile = (N_KEYS // pack, pack, LANES)
                for hd in range(PEER_HEADS):
                    n_b = jnp.broadcast_to(n_t[hd][ii:ii + 1, :], (pack, LANES)).astype(gd)[None]
                    c_b = jnp.broadcast_to(c_t[hd][ii:ii + 1, :], (pack, LANES)).astype(gd)[None]
                    r1 = pltpu.bitcast(r1_ref[hd * krows:(hd + 1) * krows, sl], gd).reshape(tile)
                    e1 = pltpu.bitcast(e1_ref[hd * krows:(hd + 1) * krows, sl], gd).reshape(tile)
                    term = jnp.where(r1 < n_b, e1, jnp.zeros_like(e1)) * c_b
                    g = term if g is None else g + term
                act = _twice_gelu_tanh(a_ref[ii * N_KEYS:(ii + 1) * N_KEYS, sl])
                gate = g.reshape(N_KEYS, LANES).astype(jnp.float32)
                w_ref[ii * N_KEYS:(ii + 1) * N_KEYS, sl] = (act * gate).astype(w_ref.dtype)
        hs = slice(hf * half, (hf + 1) * half)
        acc_ref[:, hs] += jnp.dot(vt_ref[...], w_ref[:, hs], preferred_element_type=jnp.float32)

    @pl.when(e == pl.num_programs(1) - 1)
    def _():
        o = acc_ref[...].T + x_ref[...]
        y_ref[...] = o * lax.rsqrt(jnp.mean(o * o, axis=-1, keepdims=True) + EPS) * fg_ref[...]


def peer_and_final_norm(x, norm2_g, w_q, sub_keys, u, v, final_g):
    T, D = x.shape
    assert T % MIX_TOKENS == 0 and T % SEL_TOKENS == 0 and N_KEYS % MIX_KEYS == 0 and MIX_KEYS % SUBLANES == 0
    f32 = jnp.float32
    tab = jax.ShapeDtypeStruct((PEER_HEADS * N_KEYS, T), f32)
    gtab = jax.ShapeDtypeStruct((PEER_HEADS * N_KEYS // GATE_PACK, T), jnp.uint32)
    tab_spec = pl.BlockSpec((N_KEYS, SEL_TOKENS), lambda t, h: (h, t))
    gtab_spec = pl.BlockSpec((N_KEYS // GATE_PACK, SEL_TOKENS), lambda t, h: (h, t))
    h, r1, e1, n0, c0 = pl.pallas_call(
        _peer_select_kernel,
        out_shape=(jax.ShapeDtypeStruct((T, D), MM_DTYPE), gtab, gtab, tab, tab),
        grid=(T // SEL_TOKENS, PEER_HEADS),
        in_specs=[pl.BlockSpec((SEL_TOKENS, D), lambda t, h: (t, 0)),
                  pl.BlockSpec((1, D), lambda t, h: (0, 0)),
                  pl.BlockSpec((D, PEER_QDIM), lambda t, h: (0, h)),
                  pl.BlockSpec((1, 2, N_KEYS, PEER_HALF), lambda t, h: (h, 0, 0, 0))],
        out_specs=(pl.BlockSpec((SEL_TOKENS, D), lambda t, h: (t, 0)), gtab_spec, gtab_spec, tab_spec, tab_spec),
        scratch_shapes=[pltpu.VMEM((SEL_TOKENS, D), MM_DTYPE)],
        compiler_params=pltpu.CompilerParams(dimension_semantics=("parallel", "arbitrary"),
                                             vmem_limit_bytes=VMEM_LIMIT),
        name="peer_select",
    )(x, norm2_g.reshape(1, D), w_q.astype(MM_DTYPE), sub_keys.astype(MM_DTYPE))

    eb = MIX_KEYS * N_KEYS
    mix_tab = pl.BlockSpec((PEER_HEADS * N_KEYS, MIX_TOKENS), lambda t, e: (0, t))
    mix_gtab = pl.BlockSpec((PEER_HEADS * N_KEYS // GATE_PACK, MIX_TOKENS), lambda t, e: (0, t))
    tok_blk = pl.BlockSpec((MIX_TOKENS, D), lambda t, e: (t, 0))
    return pl.pallas_call(
        _peer_mix_kernel,
        out_shape=jax.ShapeDtypeStruct((T, D), f32),
        grid=(T // MIX_TOKENS, N_KEYS // MIX_KEYS),
        in_specs=[tok_blk,
                  pl.BlockSpec((eb, D), lambda t, e: (e, 0)),
                  pl.BlockSpec((D, eb), lambda t, e: (0, e)),
                  mix_gtab, mix_gtab, mix_tab, mix_tab,
                  tok_blk,
                  pl.BlockSpec((1, D), lambda t, e: (0, 0))],
        out_specs=tok_blk,
        scratch_shapes=[pltpu.VMEM((D, MIX_TOKENS), f32),
                        pltpu.VMEM((eb, MIX_TOKENS), f32),
                        pltpu.VMEM((eb, MIX_TOKENS), MM_DTYPE)],
        compiler_params=pltpu.CompilerParams(dimension_semantics=("parallel", "arbitrary"),
                                             vmem_limit_bytes=VMEM_LIMIT),
        name="peer_mix",
    )(h, u.astype(MM_DTYPE), v.T.astype(MM_DTYPE), r1, e1, n0, c0, x, final_g.reshape(1, D))


NA_BAND = NA_ROWS * GRID_W
NA_MASKED = -1e30
NA_STEP_ROWS = 2


def _na_bias_table(rpb):
    cq = jnp.arange(GRID_W)[:, None]
    ck = jnp.arange(GRID_W)[None, :]
    cs = jnp.clip(cq - NA_COLS // 2, 0, GRID_W - NA_COLS)
    valid = (ck >= cs) & (ck < cs + NA_COLS)
    cb = jnp.clip(ck - cq + NA_COLS - 1, 0, 2 * NA_COLS - 2)
    rb = jnp.arange(NA_ROWS)[:, None] + jnp.arange(NA_ROWS)[None, :]
    tab = rpb[:, rb][:, :, :, cb]
    tab = jnp.where(valid, tab, NA_MASKED)
    return jnp.transpose(tab, (1, 0, 3, 2, 4)).reshape(NA_ROWS, NA_HEADS, GRID_W, NA_BAND)


def _na_row_start(r, rows):
    return jnp.clip(r - NA_ROWS // 2, 0, rows - NA_ROWS)


def _na_kernel(q_ref, k_ref, v_ref, *rest, rows):
    tab_refs, o_ref = rest[:NA_STEP_ROWS], rest[NA_STEP_ROWS]
    lane = lax.broadcasted_iota(jnp.int32, (GRID_W, LANES), 1)
    lo = lane < NA_HEAD_DIM
    pairs = [slice(hp * LANES, (hp + 1) * LANES) for hp in range(NA_HEADS // 2)]
    bands, scores = [], []
    for j in range(NA_STEP_ROWS):
        r = pl.program_id(1) * NA_STEP_ROWS + j
        start = pl.multiple_of(_na_row_start(r, rows) * GRID_W, GRID_W)
        kb = k_ref[0, pl.ds(start, NA_BAND), :]
        bands.append(v_ref[0, pl.ds(start, NA_BAND), :])
        q = q_ref[0, j * GRID_W:(j + 1) * GRID_W, :]
        for h in range(NA_HEADS):
            qp = q[:, pairs[h // 2]]
            mine = lo if h % 2 == 0 else jnp.logical_not(lo)
            scores.append(_dot_nt(jnp.where(mine, qp, jnp.zeros_like(qp)), kb[:, pairs[h // 2]])
                          + tab_refs[j][0, h])
    probs, sums = [], []
    for sc in scores:
        p = jnp.exp(sc - jnp.max(sc, axis=-1, keepdims=True))
        sums.append(jnp.sum(p, axis=-1, keepdims=True))
        probs.append(p.astype(bands[0].dtype))
    for j in range(NA_STEP_ROWS):
        outs = []
        for hp in range(NA_HEADS // 2):
            i0 = j * NA_HEADS + 2 * hp
            o = [jnp.dot(probs[i0 + s], bands[j][:, pairs[hp]], preferred_element_type=jnp.float32) / sums[i0 + s]
                 for s in range(2)]
            outs.append(jnp.where(lo, o[0], o[1]))
        o_ref[0, j * GRID_W:(j + 1) * GRID_W, :] = jnp.concatenate(outs, axis=-1).astype(o_ref.dtype)


def neighbourhood_attention_pallas(q, k, v, rpb):
    B, L, W = q.shape
    rows = L // GRID_W
    assert rows >= NA_ROWS and rows % NA_STEP_ROWS == 0 and W == NA_HEADS * NA_HEAD_DIM and 2 * NA_HEAD_DIM == LANES
    seq = pl.BlockSpec((1, L, W), lambda b, r: (b, 0, 0))
    row = pl.BlockSpec((1, NA_STEP_ROWS * GRID_W, W), lambda b, r: (b, r, 0))

    def tab(j):
        def index(b, r):
            rr = r * NA_STEP_ROWS + j
            return (_na_row_start(rr, rows) - rr + NA_ROWS - 1, 0, 0, 0)
        return pl.BlockSpec((1, NA_HEADS, GRID_W, NA_BAND), index)

    bias = _na_bias_table(rpb)
    return pl.pallas_call(
        functools.partial(_na_kernel, rows=rows),
        out_shape=jax.ShapeDtypeStruct((B, L, W), MM_DTYPE),
        grid=(B, rows // NA_STEP_ROWS),
        in_specs=[row, seq, seq] + [tab(j) for j in range(NA_STEP_ROWS)],
        out_specs=row,
        compiler_params=pltpu.CompilerParams(dimension_semantics=("parallel", "arbitrary"),
                                             vmem_limit_bytes=VMEM_LIMIT),
        name="na_attention",
    )(q, k, v, *([bias] * NA_STEP_ROWS))


def _rotate_half(x, cos, sin_signed):
    return x * cos + pltpu.roll(x, x.shape[-1] // 2, axis=1) * sin_signed


def _ret_kernel(q_ref, k_ref, v_ref, g_ref, cos_ref, sin_ref, df_ref, db_ref, ng_ref, o_ref,
                of_ref, ob_ref, *, nc):
    C = RET_CHUNK
    mm = o_ref.dtype
    lg_f = jax.nn.log_sigmoid(df_ref[0])[0:1, :]
    lg_b = jax.nn.log_sigmoid(db_ref[0])[0:1, :]
    pos = lax.broadcasted_iota(jnp.int32, (C, C), 0).astype(jnp.float32)
    diff = pos - lax.broadcasted_iota(jnp.int32, (C, C), 1).astype(jnp.float32)
    dec_f = jnp.where(diff >= 0, jnp.exp(lg_f * jnp.maximum(diff, 0.0)), 0.0)
    dec_b = jnp.where(diff < 0, jnp.exp(lg_b * jnp.maximum(-diff, 0.0)), 0.0)
    kdec_f = jnp.exp(lg_f * (C - 1 - pos))
    qdec_f = jnp.exp(lg_f * (pos + 1))
    kdec_b = jnp.exp(lg_b * pos)
    qdec_b = jnp.exp(lg_b * (C - pos))
    step_f = jnp.exp(lg_f * C)
    step_b = jnp.exp(lg_b * C)
    scale = RET_HEAD_DIM ** -0.5

    def load(c):
        sl = pl.ds(pl.multiple_of(c * C, C), C)
        q = _rotate_half(q_ref[0, sl, :], cos_ref[sl, :], sin_ref[sl, :])
        k = _rotate_half(k_ref[0, sl, :], cos_ref[sl, :], sin_ref[sl, :]) * scale
        return sl, q, k, v_ref[0, sl, :].astype(mm)

    def body(c, carry):
        rf, rb = carry
        sl_f, q_f, k_f, v_f = load(c)
        sl_b, q_b, k_b, v_b = load(nc - 1 - c)
        att_f = _dot_nt(q_f.astype(mm), k_f.astype(mm))
        att_b = _dot_nt(q_b.astype(mm), k_b.astype(mm))
        inter_f = jnp.dot((q_f * qdec_f).astype(mm), rf.astype(mm), preferred_element_type=jnp.float32)
        inter_b = jnp.dot((q_b * qdec_b).astype(mm), rb.astype(mm), preferred_element_type=jnp.float32)
        s_f = _dot_tn((k_f * kdec_f).astype(mm), v_f)
        s_b = _dot_tn((k_b * kdec_b).astype(mm), v_b)
        of_ref[sl_f, :] = inter_f + jnp.dot((att_f * dec_f).astype(mm), v_f, preferred_element_type=jnp.float32)
        ob_ref[sl_b, :] = inter_b + jnp.dot((att_b * dec_b).astype(mm), v_b, preferred_element_type=jnp.float32)
        return step_f * rf + s_f, step_b * rb + s_b

    zero = jnp.zeros((RET_HEAD_DIM, RET_HEAD_DIM), jnp.float32)
    lax.fori_loop(0, nc, body, (zero, zero))

    def finish(c2, carry):
        for c in (2 * c2, 2 * c2 + 1):
            sl = pl.ds(pl.multiple_of(c * C, C), C)
            o = of_ref[sl, :] + ob_ref[sl, :]
            mu = jnp.mean(o, axis=-1, keepdims=True)
            var = jnp.mean(jnp.square(o - mu), axis=-1, keepdims=True)
            o = (o - mu) * lax.rsqrt(var + EPS) * ng_ref[...]
            o_ref[0, sl, :] = (jax.nn.silu(g_ref[0, sl, :]) * o).astype(mm)
        return carry

    assert nc % 2 == 0
    lax.fori_loop(0, nc // 2, finish, 0)


def retention_pallas(q, k, v, gate, decay_fwd, decay_bwd, norm_g):
    B, L, W = q.shape
    dh = RET_HEAD_DIM
    assert W == RET_HEADS * dh and dh == LANES and L % RET_CHUNK == 0
    inv = 1.0 / (ROPE_BASE ** (jnp.arange(0, dh, 2, dtype=jnp.float32) / dh))
    ang = jnp.arange(L, dtype=jnp.float32)[:, None] * inv[None, :]
    cos = jnp.concatenate([jnp.cos(ang), jnp.cos(ang)], axis=-1)
    sin = jnp.concatenate([-jnp.sin(ang), jnp.sin(ang)], axis=-1)
    bcast = lambda d: jnp.broadcast_to(d.astype(jnp.float32)[:, None, None], (RET_HEADS, SUBLANES, dh))
    seq = pl.BlockSpec((1, L, dh), lambda b, h: (b, 0, h))
    tab = pl.BlockSpec((L, dh), lambda b, h: (0, 0))
    dec = pl.BlockSpec((1, SUBLANES, dh), lambda b, h: (h, 0, 0))
    return pl.pallas_call(
        functools.partial(_ret_kernel, nc=L // RET_CHUNK),
        out_shape=jax.ShapeDtypeStruct((B, L, W), MM_DTYPE),
        grid=(B, RET_HEADS),
        in_specs=[seq, seq, seq, seq, tab, tab, dec, dec, pl.BlockSpec((1, dh), lambda b, h: (0, h))],
        out_specs=seq,
        scratch_shapes=[pltpu.VMEM((L, dh), jnp.float32), pltpu.VMEM((L, dh), jnp.float32)],
        compiler_params=pltpu.CompilerParams(dimension_semantics=("parallel", "parallel"),
                                             vmem_limit_bytes=VMEM_LIMIT),
        name="retention",
    )(q, k, v, gate, cos, sin, bcast(decay_fwd), bcast(decay_bwd), norm_g.reshape(1, W))


PROJ_TOKENS = 512
N_NA_STREAMS = 3
N_RET_STREAMS = 4


def _in_proj_kernel(x_ref, g_ref, w_ref, *out_refs):
    x = x_ref[...]
    h = (x * lax.rsqrt(jnp.mean(x * x, axis=-1, keepdims=True) + EPS) * g_ref[...]).astype(w_ref.dtype)
    for j, o_ref in enumerate(out_refs):
        o = jnp.dot(h, w_ref[:, j * NA_WIDTH:(j + 1) * NA_WIDTH], preferred_element_type=jnp.float32)
        if j == 0:
            o = o * NA_HEAD_DIM ** -0.5
        o_ref[...] = o.astype(o_ref.dtype)


def in_projection(x, norm_g, w_in):
    T, D = x.shape
    assert NA_WIDTH == RET_WIDTH and T % PROJ_TOKENS == 0
    n = N_NA_STREAMS + N_RET_STREAMS
    assert w_in.shape == (D, n * NA_WIDTH)
    tok = pl.BlockSpec((PROJ_TOKENS, D), lambda t: (t, 0))
    out = pl.BlockSpec((PROJ_TOKENS, NA_WIDTH), lambda t: (t, 0))
    return pl.pallas_call(
        _in_proj_kernel,
        out_shape=tuple(jax.ShapeDtypeStruct((T, NA_WIDTH), MM_DTYPE if j < N_NA_STREAMS else jnp.float32)
                        for j in range(n)),
        grid=(T // PROJ_TOKENS,),
        in_specs=[tok, pl.BlockSpec((1, D), lambda t: (0, 0)), pl.BlockSpec(w_in.shape, lambda t: (0, 0))],
        out_specs=tuple(out for _ in range(n)),
        compiler_params=pltpu.CompilerParams(dimension_semantics=("parallel",), vmem_limit_bytes=VMEM_LIMIT),
        name="in_projection",
    )(x, norm_g.reshape(1, D), w_in.astype(MM_DTYPE))


def _out_proj_kernel(a_ref, r_ref, w_ref, x_ref, o_ref):
    o = jnp.dot(a_ref[...], w_ref[:NA_WIDTH, :], preferred_element_type=jnp.float32)
    o = o + jnp.dot(r_ref[...], w_ref[NA_WIDTH:, :], preferred_element_type=jnp.float32)
    o_ref[...] = x_ref[...] + o


def out_projection(na_o, ret_o, w_out, x):
    T, D = x.shape
    tok = pl.BlockSpec((PROJ_TOKENS, D), lambda t: (t, 0))
    half = pl.BlockSpec((PROJ_TOKENS, NA_WIDTH), lambda t: (t, 0))
    return pl.pallas_call(
        _out_proj_kernel,
        out_shape=jax.ShapeDtypeStruct((T, D), jnp.float32),
        grid=(T // PROJ_TOKENS,),
        in_specs=[half, half, pl.BlockSpec(w_out.shape, lambda t: (0, 0)), tok],
        out_specs=tok,
        compiler_params=pltpu.CompilerParams(dimension_semantics=("parallel",), vmem_limit_bytes=VMEM_LIMIT),
        name="out_projection",
    )(na_o, ret_o, w_out.astype(MM_DTYPE), x)


def encoder_layer(x, norm1_g, w_in, na_rpb, ret_decay_fwd, ret_decay_bwd, ret_norm_g,
                  w_out, norm2_g, peer_w_q, peer_sub_keys, peer_u, peer_v, final_g):
    B, L, D = x.shape
    xt = x.reshape(B * L, D)
    streams = [s.reshape(B, L, NA_WIDTH) for s in in_projection(xt, norm1_g, w_in)]
    na_o = neighbourhood_attention_pallas(*streams[:N_NA_STREAMS], na_rpb)
    ret_o = retention_pallas(*streams[N_NA_STREAMS:], ret_decay_fwd, ret_decay_bwd, ret_norm_g)
    x1 = out_projection(na_o.reshape(B * L, NA_WIDTH), ret_o.reshape(B * L, RET_WIDTH), w_out, xt)
    y = peer_and_final_norm(x1, norm2_g, peer_w_q, peer_sub_keys, peer_u, peer_v, final_g)
    return y.reshape(B, L, D)


def kernel(x_prompt, x_sample, norm1_g, w_in, na_rpb, ret_decay_fwd, ret_decay_bwd, ret_norm_g,
           w_out, norm2_g, peer_w_q, peer_sub_keys, peer_u, peer_v, final_g):
    assert DEPTH == 1

    def trunk(x):
        return encoder_layer(x, norm1_g[0], w_in[0], na_rpb[0], ret_decay_fwd[0], ret_decay_bwd[0],
                             ret_norm_g[0], w_out[0], norm2_g[0], peer_w_q[0], peer_sub_keys[0],
                             peer_u[0], peer_v[0], final_g)

    return (trunk(x_prompt), trunk(x_sample))
```

```python
import functools
import math
import jax, jax.numpy as jnp
from jax import lax
from jax.experimental import pallas as pl
from jax.experimental.pallas import tpu as pltpu

D_MODEL = 1024
DEPTH = 1
GRID_W = 64
NA_HEADS = 8
NA_HEAD_DIM = 64
NA_WIDTH = NA_HEADS * NA_HEAD_DIM
NA_ROWS = 8
NA_COLS = 16
RET_HEADS = 4
RET_HEAD_DIM = 128
RET_WIDTH = RET_HEADS * RET_HEAD_DIM
RET_CHUNK = 128
ROPE_BASE = 10000.0
PEER_HEADS = 8
PEER_QDIM = 256
PEER_HALF = PEER_QDIM // 2
N_KEYS = 128
N_EXPERTS = N_KEYS * N_KEYS
PEER_TOPK = 16
EPS = 1e-6

LANES = 128
SUBLANES = 8
VMEM_LIMIT = 56 * 1024 * 1024
MM_DTYPE = jnp.bfloat16


def _dot_nt(a, b):
    return lax.dot_general(a, b, (((1,), (1,)), ((), ())), preferred_element_type=jnp.float32)


def _dot_tn(a, b):
    return lax.dot_general(a, b, (((0,), (0,)), ((), ())), preferred_element_type=jnp.float32)


GATE_DTYPE = jnp.bfloat16
GATE_PACK = 2
NOT_TOP = 255.0
SEL_TOKENS = 1024
MIX_TOKENS = 512
MIX_KEYS = 16


def _topk_rank(s):
    iota = lax.broadcasted_iota(jnp.int32, s.shape, 0).astype(jnp.float32)
    rank = jnp.full(s.shape, NOT_TOP, jnp.float32)
    tops = []
    for r in range(PEER_TOPK):
        m = jnp.max(s, axis=0, keepdims=True)
        idx = jnp.min(jnp.where(s == m, iota, float(s.shape[0])), axis=0, keepdims=True)
        hit = iota == idx
        rank = jnp.where(hit, float(r), rank)
        s = jnp.where(hit, -jnp.inf, s)
        tops.append(m)
    return rank, tops


def _topk_rank_untied(s):
    rank = jnp.full(s.shape, NOT_TOP, jnp.float32)
    tops = []
    for r in range(PEER_TOPK):
        m = jnp.max(s, axis=0, keepdims=True)
        hit = s == m
        rank = jnp.where(hit, float(r), rank)
        s = jnp.where(hit, -jnp.inf, s)
        tops.append(m)
    removed = jnp.sum(jnp.where(rank < NOT_TOP, 1.0, 0.0), axis=0, keepdims=True)
    return rank, tops, removed


def _young_counts(a_top, b_top):
    k = PEER_TOPK
    w = a_top[0].shape[1]
    b16 = jnp.concatenate(b_top, axis=0)
    b8 = jnp.concatenate(b_top[:8], axis=0)
    blocks = [a_top[0] + b16] + [a_top[a] + b8 for a in range(1, 8)]
    blocks.append(jnp.concatenate(a_top[8:], axis=0) + b_top[0])
    cand = jnp.concatenate(blocks, axis=0)
    rows = cand.shape[0]
    rho = lax.broadcasted_iota(jnp.int32, (rows, w), 0)
    mid = rho - 16
    a_id = jnp.where(rho < 16, 0, jnp.where(rho < 72, 1 + (mid >> 3), rho - 64))
    b_id = jnp.where(rho < 16, rho, jnp.where(rho < 72, mid & 7, 0))
    pos = (a_id * k + b_id).astype(jnp.float32)
    cand = jnp.where((a_id + 1) * (b_id + 1) <= k, cand, -jnp.inf)
    a_iota = lax.broadcasted_iota(jnp.int32, (k, w), 0).astype(jnp.float32)
    n = jnp.zeros((k, w), jnp.float32)
    best = []
    for _ in range(k):
        m = jnp.max(cand, axis=0, keepdims=True)
        p = jnp.min(jnp.where(cand == m, pos, float(k * k)), axis=0, keepdims=True)
        cand = jnp.where(pos == p, -jnp.inf, cand)
        n = n + (a_iota == jnp.floor(p * (1.0 / k))).astype(jnp.float32)
        best.append(m)
    return n, best


def _peer_select_kernel(x_ref, g_ref, wq_ref, keys_ref, h_ref, r1_ref, e1_ref, n0_ref, c0_ref, hb_ref):
    hd = pl.program_id(1)

    @pl.when(hd == 0)
    def _():
        x = x_ref[...]
        y = x * lax.rsqrt(jnp.mean(x * x, axis=-1, keepdims=True) + EPS) * g_ref[...]
        hb_ref[...] = y.astype(hb_ref.dtype)
        h_ref[...] = y.astype(h_ref.dtype)

    def emit(c, s0, s1, rank0, a_top, rank1, b_top):
        n, best = _young_counts(a_top, b_top)
        z = jnp.zeros_like(best[0])
        for m in best:
            z = z + jnp.exp(m - best[0])
        n0 = jnp.zeros_like(s0)
        for a in range(PEER_TOPK):
            n0 = jnp.where(rank0 == float(a), n[a:a + 1, :], n0)
        sl = slice(c * LANES, (c + 1) * LANES)
        e1 = jnp.where(rank1 < NOT_TOP, jnp.exp(s1 - b_top[0]), 0.0)
        r1_ref[:, sl] = pltpu.bitcast(rank1.astype(GATE_DTYPE), r1_ref.dtype)
        e1_ref[:, sl] = pltpu.bitcast(e1.astype(GATE_DTYPE), e1_ref.dtype)
        n0_ref[:, sl] = n0
        c0_ref[:, sl] = jnp.exp(s0 - a_top[0]) * (0.5 / z)

    chunks = range(SEL_TOKENS // LANES)
    scores = []
    qs = [jnp.dot(hb_ref[c * LANES:(c + 1) * LANES, :], wq_ref[...],
                  preferred_element_type=jnp.float32).astype(keys_ref.dtype) for c in chunks]
    for qc in qs:
        scores.append((_dot_nt(keys_ref[0, 0], qc[:, :PEER_HALF]),
                       _dot_nt(keys_ref[0, 1], qc[:, PEER_HALF:])))
    untied = [(_topk_rank_untied(s0), _topk_rank_untied(s1)) for s0, s1 in scores]
    removed = functools.reduce(jnp.maximum, [t[2] for pair in untied for t in pair])
    tied = jnp.max(removed) > float(PEER_TOPK)

    @pl.when(tied)
    def _():
        for c, (s0, s1) in zip(chunks, scores):
            emit(c, s0, s1, *_topk_rank(s0), *_topk_rank(s1))

    @pl.when(jnp.logical_not(tied))
    def _():
        for c, (s0, s1), (t0, t1) in zip(chunks, scores, untied):
            emit(c, s0, s1, t0[0], t0[1], t1[0], t1[1])


def _twice_gelu_tanh(x):
    c = math.sqrt(2.0 / math.pi)
    return x * (1.0 + jnp.tanh(x * (c + (c * 0.044715) * (x * x))))


def _peer_mix_kernel(h_ref, u_ref, vt_ref, r1_ref, e1_ref, n0_ref, c0_ref, x_ref, fg_ref, y_ref,
                     acc_ref, a_ref, w_ref):
    e = pl.program_id(1)

    @pl.when(e == 0)
    def _():
        acc_ref[...] = jnp.zeros_like(acc_ref)

    gd = GATE_DTYPE
    pack = SUBLANES * GATE_PACK
    krows = N_KEYS // GATE_PACK
    half = MIX_TOKENS // 2
    for hf in range(2):
        a_ref[:, hf * half:(hf + 1) * half] = _dot_nt(u_ref[...], h_ref[hf * half:(hf + 1) * half, :])
    rows = [pl.ds(pl.multiple_of(hd * N_KEYS + e * MIX_KEYS, MIX_KEYS), MIX_KEYS) for hd in range(PEER_HEADS)]
    for hf in range(2):
        for c in range(hf * half // LANES, (hf + 1) * half // LANES):
            sl = slice(c * LANES, (c + 1) * LANES)
            n_t = [n0_ref[r, sl] for r in rows]
            c_t = [c0_ref[r, sl] for r in rows]
            for ii in range(MIX_KEYS):
                g = None
                tile = (N_KEYS // pack, pack, LANES)
                for hd in range(PEER_HEADS):
                    n_b = jnp.broadcast_to(n_t[hd][ii:ii + 1, :], (pack, LANES)).astype(gd)[None]
                    c_b = jnp.broadcast_to(c_t[hd][ii:ii + 1, :], (pack, LANES)).astype(gd)[None]
                    r1 = pltpu.bitcast(r1_ref[hd * krows:(hd + 1) * krows, sl], gd).reshape(tile)
                    e1 = pltpu.bitcast(e1_ref[hd * krows:(hd + 1) * krows, sl], gd).reshape(tile)
                    term = jnp.where(r1 < n_b, e1, jnp.zeros_like(e1)) * c_b
                    g = term if g is None else g + term
                act = _twice_gelu_tanh(a_ref[ii * N_KEYS:(ii + 1) * N_KEYS, sl])
                gate = g.reshape(N_KEYS, LANES).astype(jnp.float32)
                w_ref[ii * N_KEYS:(ii + 1) * N_KEYS, sl] = (act * gate).astype(w_ref.dtype)
        hs = slice(hf * half, (hf + 1) * half)
        acc_ref[:, hs] += jnp.dot(vt_ref[...], w_ref[:, hs], preferred_element_type=jnp.float32)

    @pl.when(e == pl.num_programs(1) - 1)
    def _():
        o = acc_ref[...].T + x_ref[...]
        y_ref[...] = o * lax.rsqrt(jnp.mean(o * o, axis=-1, keepdims=True) + EPS) * fg_ref[...]


def peer_and_final_norm(x, norm2_g, w_q, sub_keys, u, v, final_g):
    T, D = x.shape
    assert T % MIX_TOKENS == 0 and T % SEL_TOKENS == 0 and N_KEYS % MIX_KEYS == 0 and MIX_KEYS % SUBLANES == 0
    f32 = jnp.float32
    tab = jax.ShapeDtypeStruct((PEER_HEADS * N_KEYS, T), f32)
    gtab = jax.ShapeDtypeStruct((PEER_HEADS * N_KEYS // GATE_PACK, T), jnp.uint32)
    tab_spec = pl.BlockSpec((N_KEYS, SEL_TOKENS), lambda t, h: (h, t))
    gtab_spec = pl.BlockSpec((N_KEYS // GATE_PACK, SEL_TOKENS), lambda t, h: (h, t))
    h, r1, e1, n0, c0 = pl.pallas_call(
        _peer_select_kernel,
        out_shape=(jax.ShapeDtypeStruct((T, D), MM_DTYPE), gtab, gtab, tab, tab),
        grid=(T // SEL_TOKENS, PEER_HEADS),
        in_specs=[pl.BlockSpec((SEL_TOKENS, D), lambda t, h: (t, 0)),
                  pl.BlockSpec((1, D), lambda t, h: (0, 0)),
                  pl.BlockSpec((D, PEER_QDIM), lambda t, h: (0, h)),
                  pl.BlockSpec((1, 2, N_KEYS, PEER_HALF), lambda t, h: (h, 0, 0, 0))],
        out_specs=(pl.BlockSpec((SEL_TOKENS, D), lambda t, h: (t, 0)), gtab_spec, gtab_spec, tab_spec, tab_spec),
        scratch_shapes=[pltpu.VMEM((SEL_TOKENS, D), MM_DTYPE)],
        compiler_params=pltpu.CompilerParams(dimension_semantics=("parallel", "arbitrary"),
                                             vmem_limit_bytes=VMEM_LIMIT),
        name="peer_select",
    )(x, norm2_g.reshape(1, D), w_q.astype(MM_DTYPE), sub_keys.astype(MM_DTYPE))

    eb = MIX_KEYS * N_KEYS
    mix_tab = pl.BlockSpec((PEER_HEADS * N_KEYS, MIX_TOKENS), lambda t, e: (0, t))
    mix_gtab = pl.BlockSpec((PEER_HEADS * N_KEYS // GATE_PACK, MIX_TOKENS), lambda t, e: (0, t))
    tok_blk = pl.BlockSpec((MIX_TOKENS, D), lambda t, e: (t, 0))
    return pl.pallas_call(
        _peer_mix_kernel,
        out_shape=jax.ShapeDtypeStruct((T, D), f32),
        grid=(T // MIX_TOKENS, N_KEYS // MIX_KEYS),
        in_specs=[tok_blk,
                  pl.BlockSpec((eb, D), lambda t, e: (e, 0)),
                  pl.BlockSpec((D, eb), lambda t, e: (0, e)),
                  mix_gtab, mix_gtab, mix_tab, mix_tab,
                  tok_blk,
                  pl.BlockSpec((1, D), lambda t, e: (0, 0))],
        out_specs=tok_blk,
        scratch_shapes=[pltpu.VMEM((D, MIX_TOKENS), f32),
                        pltpu.VMEM((eb, MIX_TOKENS), f32),
                        pltpu.VMEM((eb, MIX_TOKENS), MM_DTYPE)],
        compiler_params=pltpu.CompilerParams(dimension_semantics=("parallel", "arbitrary"),
                                             vmem_limit_bytes=VMEM_LIMIT),
        name="peer_mix",
    )(h, u.astype(MM_DTYPE), v.T.astype(MM_DTYPE), r1, e1, n0, c0, x, final_g.reshape(1, D))


NA_BAND = NA_ROWS * GRID_W
NA_MASKED = -1e30
NA_STEP_ROWS = 4


def _na_bias_table(rpb):
    cq = jnp.arange(GRID_W)[:, None]
    ck = jnp.arange(GRID_W)[None, :]
    cs = jnp.clip(cq - NA_COLS // 2, 0, GRID_W - NA_COLS)
    valid = (ck >= cs) & (ck < cs + NA_COLS)
    cb = jnp.clip(ck - cq + NA_COLS - 1, 0, 2 * NA_COLS - 2)
    rb = jnp.arange(NA_ROWS)[:, None] + jnp.arange(NA_ROWS)[None, :]
    tab = rpb[:, rb][:, :, :, cb]
    tab = jnp.where(valid, tab, NA_MASKED)
    return jnp.transpose(tab, (1, 0, 3, 2, 4)).reshape(NA_ROWS, NA_HEADS, GRID_W, NA_BAND)


def _na_row_start(r, rows):
    return jnp.clip(r - NA_ROWS // 2, 0, rows - NA_ROWS)


def _na_kernel(q_ref, k_ref, v_ref, *rest, rows):
    tab_refs, o_ref = rest[:NA_STEP_ROWS], rest[NA_STEP_ROWS]
    lane = lax.broadcasted_iota(jnp.int32, (GRID_W, LANES), 1)
    lo = lane < NA_HEAD_DIM
    pairs = [slice(hp * LANES, (hp + 1) * LANES) for hp in range(NA_HEADS // 2)]
    bands, scores = [], []
    for j in range(NA_STEP_ROWS):
        r = pl.program_id(1) * NA_STEP_ROWS + j
        start = pl.multiple_of(_na_row_start(r, rows) * GRID_W, GRID_W)
        kb = k_ref[0, pl.ds(start, NA_BAND), :]
        bands.append(v_ref[0, pl.ds(start, NA_BAND), :])
        q = q_ref[0, j * GRID_W:(j + 1) * GRID_W, :]
        for h in range(NA_HEADS):
            qp = q[:, pairs[h // 2]]
            mine = lo if h % 2 == 0 else jnp.logical_not(lo)
            scores.append(_dot_nt(jnp.where(mine, qp, jnp.zeros_like(qp)), kb[:, pairs[h // 2]])
                          + tab_refs[j][0, h])
    probs, sums = [], []
    for sc in scores:
        p = jnp.exp(sc - jnp.max(sc, axis=-1, keepdims=True))
        sums.append(jnp.sum(p, axis=-1, keepdims=True))
        probs.append(p.astype(bands[0].dtype))
    for j in range(NA_STEP_ROWS):
        outs = []
        for hp in range(NA_HEADS // 2):
            i0 = j * NA_HEADS + 2 * hp
            o = [jnp.dot(probs[i0 + s], bands[j][:, pairs[hp]], preferred_element_type=jnp.float32) / sums[i0 + s]
                 for s in range(2)]
            outs.append(jnp.where(lo, o[0], o[1]))
        o_ref[0, j * GRID_W:(j + 1) * GRID_W, :] = jnp.concatenate(outs, axis=-1).astype(o_ref.dtype)


def neighbourhood_attention_pallas(q, k, v, rpb):
    B, L, W = q.shape
    rows = L // GRID_W
    assert rows >= NA_ROWS and rows % NA_STEP_ROWS == 0 and W == NA_HEADS * NA_HEAD_DIM and 2 * NA_HEAD_DIM == LANES
    seq = pl.BlockSpec((1, L, W), lambda b, r: (b, 0, 0))
    row = pl.BlockSpec((1, NA_STEP_ROWS * GRID_W, W), lambda b, r: (b, r, 0))

    def tab(j):
        def index(b, r):
            rr = r * NA_STEP_ROWS + j
            return (_na_row_start(rr, rows) - rr + NA_ROWS - 1, 0, 0, 0)
        return pl.BlockSpec((1, NA_HEADS, GRID_W, NA_BAND), index)

    bias = _na_bias_table(rpb)
    return pl.pallas_call(
        functools.partial(_na_kernel, rows=rows),
        out_shape=jax.ShapeDtypeStruct((B, L, W), MM_DTYPE),
        grid=(B, rows // NA_STEP_ROWS),
        in_specs=[row, seq, seq] + [tab(j) for j in range(NA_STEP_ROWS)],
        out_specs=row,
        compiler_params=pltpu.CompilerParams(dimension_semantics=("parallel", "arbitrary"),
                                             vmem_limit_bytes=VMEM_LIMIT),
        name="na_attention",
    )(q, k, v, *([bias] * NA_STEP_ROWS))


def _rotate_half(x, cos, sin_signed):
    return x * cos + pltpu.roll(x, x.shape[-1] // 2, axis=1) * sin_signed


def _ret_kernel(q_ref, k_ref, v_ref, g_ref, cos_ref, sin_ref, df_ref, db_ref, ng_ref, o_ref,
                of_ref, ob_ref, *, nc):
    C = RET_CHUNK
    mm = o_ref.dtype
    lg_f = jax.nn.log_sigmoid(df_ref[0])[0:1, :]
    lg_b = jax.nn.log_sigmoid(db_ref[0])[0:1, :]
    pos = lax.broadcasted_iota(jnp.int32, (C, C), 0).astype(jnp.float32)
    diff = pos - lax.broadcasted_iota(jnp.int32, (C, C), 1).astype(jnp.float32)
    dec_f = jnp.where(diff >= 0, jnp.exp(lg_f * jnp.maximum(diff, 0.0)), 0.0)
    dec_b = jnp.where(diff < 0, jnp.exp(lg_b * jnp.maximum(-diff, 0.0)), 0.0)
    kdec_f = jnp.exp(lg_f * (C - 1 - pos))
    qdec_f = jnp.exp(lg_f * (pos + 1))
    kdec_b = jnp.exp(lg_b * pos)
    qdec_b = jnp.exp(lg_b * (C - pos))
    step_f = jnp.exp(lg_f * C)
    step_b = jnp.exp(lg_b * C)
    scale = RET_HEAD_DIM ** -0.5

    def load(c):
        sl = pl.ds(pl.multiple_of(c * C, C), C)
        q = _rotate_half(q_ref[0, sl, :], cos_ref[sl, :], sin_ref[sl, :])
        k = _rotate_half(k_ref[0, sl, :], cos_ref[sl, :], sin_ref[sl, :]) * scale
        return sl, q, k, v_ref[0, sl, :].astype(mm)

    def body(c, carry):
        rf, rb = carry
        sl_f, q_f, k_f, v_f = load(c)
        sl_b, q_b, k_b, v_b = load(nc - 1 - c)
        att_f = _dot_nt(q_f.astype(mm), k_f.astype(mm))
        att_b = _dot_nt(q_b.astype(mm), k_b.astype(mm))
        inter_f = jnp.dot((q_f * qdec_f).astype(mm), rf.astype(mm), preferred_element_type=jnp.float32)
        inter_b = jnp.dot((q_b * qdec_b).astype(mm), rb.astype(mm), preferred_element_type=jnp.float32)
        s_f = _dot_tn((k_f * kdec_f).astype(mm), v_f)
        s_b = _dot_tn((k_b * kdec_b).astype(mm), v_b)
        of_ref[sl_f, :] = inter_f + jnp.dot((att_f * dec_f).astype(mm), v_f, preferred_element_type=jnp.float32)
        ob_ref[sl_b, :] = inter_b + jnp.dot((att_b * dec_b).astype(mm), v_b, preferred_element_type=jnp.float32)
        return step_f * rf + s_f, step_b * rb + s_b

    def body2(c2, carry):
        return body(2 * c2 + 1, body(2 * c2, carry))

    zero = jnp.zeros((RET_HEAD_DIM, RET_HEAD_DIM), jnp.float32)
    assert nc % 2 == 0
    lax.fori_loop(0, nc // 2, body2, (zero, zero))

    def finish(c2, carry):
        for c in (2 * c2, 2 * c2 + 1):
            sl = pl.ds(pl.multiple_of(c * C, C), C)
            o = of_ref[sl, :] + ob_ref[sl, :]
            mu = jnp.mean(o, axis=-1, keepdims=True)
            var = jnp.mean(jnp.square(o - mu), axis=-1, keepdims=True)
            o = (o - mu) * lax.rsqrt(var + EPS) * ng_ref[...]
            o_ref[0, sl, :] = (jax.nn.silu(g_ref[0, sl, :]) * o).astype(mm)
        return carry

    lax.fori_loop(0, nc // 2, finish, 0)


def retention_pallas(q, k, v, gate, decay_fwd, decay_bwd, norm_g):
    B, L, W = q.shape
    dh = RET_HEAD_DIM
    assert W == RET_HEADS * dh and dh == LANES and L % RET_CHUNK == 0
    inv = 1.0 / (ROPE_BASE ** (jnp.arange(0, dh, 2, dtype=jnp.float32) / dh))
    ang = jnp.arange(L, dtype=jnp.float32)[:, None] * inv[None, :]
    cos = jnp.concatenate([jnp.cos(ang), jnp.cos(ang)], axis=-1)
    sin = jnp.concatenate([-jnp.sin(ang), jnp.sin(ang)], axis=-1)
    bcast = lambda d: jnp.broadcast_to(d.astype(jnp.float32)[:, None, None], (RET_HEADS, SUBLANES, dh))
    seq = pl.BlockSpec((1, L, dh), lambda b, h: (b, 0, h))
    tab = pl.BlockSpec((L, dh), lambda b, h: (0, 0))
    dec = pl.BlockSpec((1, SUBLANES, dh), lambda b, h: (h, 0, 0))
    return pl.pallas_call(
        functools.partial(_ret_kernel, nc=L // RET_CHUNK),
        out_shape=jax.ShapeDtypeStruct((B, L, W), MM_DTYPE),
        grid=(B, RET_HEADS),
        in_specs=[seq, seq, seq, seq, tab, tab, dec, dec, pl.BlockSpec((1, dh), lambda b, h: (0, h))],
        out_specs=seq,
        scratch_shapes=[pltpu.VMEM((L, dh), jnp.float32), pltpu.VMEM((L, dh), jnp.float32)],
        compiler_params=pltpu.CompilerParams(dimension_semantics=("parallel", "parallel"),
                                             vmem_limit_bytes=VMEM_LIMIT),
        name="retention",
    )(q, k, v, gate, cos, sin, bcast(decay_fwd), bcast(decay_bwd), norm_g.reshape(1, W))


PROJ_TOKENS = 512
N_NA_STREAMS = 3
N_RET_STREAMS = 4


def _in_proj_kernel(x_ref, g_ref, w_ref, *out_refs):
    x = x_ref[...]
    h = (x * lax.rsqrt(jnp.mean(x * x, axis=-1, keepdims=True) + EPS) * g_ref[...]).astype(w_ref.dtype)
    for j, o_ref in enumerate(out_refs):
        o = jnp.dot(h, w_ref[:, j * NA_WIDTH:(j + 1) * NA_WIDTH], preferred_element_type=jnp.float32)
        if j == 0:
            o = o * NA_HEAD_DIM ** -0.5
        o_ref[...] = o.astype(o_ref.dtype)


def in_projection(x, norm_g, w_in):
    T, D = x.shape
    assert NA_WIDTH == RET_WIDTH and T % PROJ_TOKENS == 0
    n = N_NA_STREAMS + N_RET_STREAMS
    assert w_in.shape == (D, n * NA_WIDTH)
    tok = pl.BlockSpec((PROJ_TOKENS, D), lambda t: (t, 0))
    out = pl.BlockSpec((PROJ_TOKENS, NA_WIDTH), lambda t: (t, 0))
    return pl.pallas_call(
        _in_proj_kernel,
        out_shape=tuple(jax.ShapeDtypeStruct((T, NA_WIDTH), MM_DTYPE if j < N_NA_STREAMS else jnp.float32)
                        for j in range(n)),
        grid=(T // PROJ_TOKENS,),
        in_specs=[tok, pl.BlockSpec((1, D), lambda t: (0, 0)), pl.BlockSpec(w_in.shape, lambda t: (0, 0))],
        out_specs=tuple(out for _ in range(n)),
        compiler_params=pltpu.CompilerParams(dimension_semantics=("parallel",), vmem_limit_bytes=VMEM_LIMIT),
        name="in_projection",
    )(x, norm_g.reshape(1, D), w_in.astype(MM_DTYPE))


def _out_proj_kernel(a_ref, r_ref, w_ref, x_ref, o_ref):
    o = jnp.dot(a_ref[...], w_ref[:NA_WIDTH, :], preferred_element_type=jnp.float32)
    o = o + jnp.dot(r_ref[...], w_ref[NA_WIDTH:, :], preferred_element_type=jnp.float32)
    o_ref[...] = x_ref[...] + o


def out_projection(na_o, ret_o, w_out, x):
    T, D = x.shape
    tok = pl.BlockSpec((PROJ_TOKENS, D), lambda t: (t, 0))
    half = pl.BlockSpec((PROJ_TOKENS, NA_WIDTH), lambda t: (t, 0))
    return pl.pallas_call(
        _out_proj_kernel,
        out_shape=jax.ShapeDtypeStruct((T, D), jnp.float32),
        grid=(T // PROJ_TOKENS,),
        in_specs=[half, half, pl.BlockSpec(w_out.shape, lambda t: (0, 0)), tok],
        out_specs=tok,
        compiler_params=pltpu.CompilerParams(dimension_semantics=("parallel",), vmem_limit_bytes=VMEM_LIMIT),
        name="out_projection",
    )(na_o, ret_o, w_out.astype(MM_DTYPE), x)


def encoder_layer(x, norm1_g, w_in, na_rpb, ret_decay_fwd, ret_decay_bwd, ret_norm_g,
                  w_out, norm2_g, peer_w_q, peer_sub_keys, peer_u, peer_v, final_g):
    B, L, D = x.shape
    xt = x.reshape(B * L, D)
    streams = [s.reshape(B, L, NA_WIDTH) for s in in_projection(xt, norm1_g, w_in)]
    na_o = neighbourhood_attention_pallas(*streams[:N_NA_STREAMS], na_rpb)
    ret_o = retention_pallas(*streams[N_NA_STREAMS:], ret_decay_fwd, ret_decay_bwd, ret_norm_g)
    x1 = out_projection(na_o.reshape(B * L, NA_WIDTH), ret_o.reshape(B * L, RET_WIDTH), w_out, xt)
    y = peer_and_final_norm(x1, norm2_g, peer_w_q, peer_sub_keys, peer_u, peer_v, final_g)
    return y.reshape(B, L, D)


def kernel(x_prompt, x_sample, norm1_g, w_in, na_rpb, ret_decay_fwd, ret_decay_bwd, ret_norm_g,
           w_out, norm2_g, peer_w_q, peer_sub_keys, peer_u, peer_v, final_g):
    assert DEPTH == 1

    def trunk(x):
        return encoder_layer(x, norm1_g[0], w_in[0], na_rpb[0], ret_decay_fwd[0], ret_decay_bwd[0],
                             ret_norm_g[0], w_out[0], norm2_g[0], peer_w_q[0], peer_sub_keys[0],
                             peer_u[0], peer_v[0], final_g)

    return (trunk(x_prompt), trunk(x_sample))
```
